```python
import jax, jax.numpy as jnp
from jax import lax
import numpy as np

D_MODEL = 1024
BATCH = 8
SEQ = 2048
DEPTH = 1

PLE_DIM = 256
SG_GROUPS = 4
SG_CHUNK = 128
SG_WIDTH = D_MODEL // 2
SG_GROUP_DIM = SG_WIDTH // SG_GROUPS
GLA_HEADS = 4
GLA_DK = D_MODEL // 2
GLA_DV = D_MODEL
GLA_HEAD_K = GLA_DK // GLA_HEADS
GLA_HEAD_V = GLA_DV // GLA_HEADS
GLA_GATE_RANK = 16
GLA_GATE_TEMP = 16.0
GLA_CHUNK = 64
D_FF = 4 * D_MODEL
EPS = 1e-6

IN_SPLITS = (SG_WIDTH, SG_WIDTH, GLA_DK, GLA_DK, GLA_DV, GLA_GATE_RANK, GLA_DV, D_MODEL, D_MODEL)
IN_COLS = SG_WIDTH * 2 + GLA_DK * 2 + GLA_DV * 2 + GLA_GATE_RANK + D_MODEL * 2

kernel_name = "hybrid_gated_sgmlp_gla_block"


def rms_norm(x, g):
    xf = x.astype(jnp.float32)
    y = xf * lax.rsqrt(jnp.mean(xf * xf, axis=-1, keepdims=True) + EPS)
    return (y * g.astype(jnp.float32)).astype(x.dtype)


def layer_norm(x, g, b):
    xf = x.astype(jnp.float32)
    mu = jnp.mean(xf, axis=-1, keepdims=True)
    xc = xf - mu
    var = jnp.mean(xc * xc, axis=-1, keepdims=True)
    y = xc * lax.rsqrt(var + EPS) * g.astype(jnp.float32) + b.astype(jnp.float32)
    return y.astype(x.dtype)


def split_cols(t):
    idx = [int(i) for i in np.cumsum(np.array(IN_SPLITS))[:-1]]
    return jnp.split(t, idx, axis=-1)


def spatial_gating(u, v, w_s, b_s):
    B, S, _ = u.shape
    n = S // SG_CHUNK
    mask = jnp.tril(jnp.ones((SG_CHUNK, SG_CHUNK), dtype=bool))
    w = jnp.where(mask[None], w_s, jnp.zeros_like(w_s)).astype(v.dtype)
    vc = v.reshape(B, n, SG_CHUNK, SG_GROUPS, SG_GROUP_DIM)
    mixed = jnp.einsum('gts,bnsgc->bntgc', w, vc) + b_s.T.astype(v.dtype)[None, None, :, :, None]
    return u * mixed.reshape(B, S, SG_WIDTH)


def gla_chunked(q, k, v, log_a):
    B, S, H, dk = q.shape
    dv = v.shape[-1]
    C = GLA_CHUNK
    n = S // C

    def to_chunks(t):
        return t.astype(jnp.float32).reshape(B, n, C, H, t.shape[-1]).transpose(0, 1, 3, 2, 4)

    q, k, v, log_a = to_chunks(q), to_chunks(k), to_chunks(v), to_chunks(log_a)
    b = jnp.cumsum(log_a, axis=3)
    b_ref = b[:, :, :, C // 2 - 1:C // 2, :]
    q_in = q * jnp.exp(b - b_ref)
    k_in = k * jnp.exp(b_ref - b)
    scores = jnp.einsum('bnhid,bnhjd->bnhij', q_in, k_in)
    causal = jnp.tril(jnp.ones((C, C), dtype=bool))
    scores = jnp.where(causal, scores, 0.0)
    o_intra = jnp.einsum('bnhij,bnhjv->bnhiv', scores, v)

    b_last = b[:, :, :, -1:, :]
    k_dec = k * jnp.exp(b_last - b)
    chunk_kv = jnp.einsum('bnhjd,bnhjv->bnhdv', k_dec, v)
    chunk_decay = jnp.exp(b_last[:, :, :, 0, :])

    def step(state, inp):
        kv, dec = inp
        return dec[..., None] * state + kv, state

    s0 = jnp.zeros((B, H, dk, dv), jnp.float32)
    _, s_prev = lax.scan(step, s0, (chunk_kv.transpose(1, 0, 2, 3, 4), chunk_decay.transpose(1, 0, 2, 3)))
    s_prev = s_prev.transpose(1, 0, 2, 3, 4)
    o_inter = jnp.einsum('bnhid,bnhdv->bnhiv', q * jnp.exp(b), s_prev)
    o = o_intra + o_inter
    return o.transpose(0, 1, 3, 2, 4).reshape(B, S, H, dv)


def setup_inputs(seed: int = 0) -> dict:
    key = jax.random.key(seed)
    ks = jax.random.split(key, 24)
    L = DEPTH
    f32 = jnp.float32

    def nrm(k, shape, scale):
        return jax.random.normal(k, shape, f32) * scale

    def gain(k, shape):
        return 1.0 + 0.02 * jax.random.normal(k, shape, f32)

    return {
        "x": jax.random.normal(ks[0], (BATCH, SEQ, D_MODEL), f32),
        "p": jax.random.normal(ks[1], (DEPTH, BATCH, SEQ, PLE_DIM), f32),
        "norm_mix_g": gain(ks[2], (L, D_MODEL)),
        "w_in": nrm(ks[3], (L, D_MODEL, IN_COLS), D_MODEL ** -0.5),
        "sg_ln_g": gain(ks[4], (L, SG_WIDTH)),
        "sg_ln_b": nrm(ks[5], (L, SG_WIDTH), 0.02),
        "sg_w_s": nrm(ks[6], (L, SG_GROUPS, SG_CHUNK, SG_CHUNK), SG_CHUNK ** -0.5),
        "sg_b_s": gain(ks[7], (L, SG_GROUPS, SG_CHUNK)),
        "sg_w_out": nrm(ks[8], (L, SG_WIDTH, D_MODEL), SG_WIDTH ** -0.5),
        "gla_w_gate_up": nrm(ks[9], (L, GLA_GATE_RANK, GLA_DK), GLA_GATE_RANK ** -0.5),
        "gla_b_gate": nrm(ks[10], (L, GLA_DK), 0.01),
        "gla_norm_g": gain(ks[11], (L, GLA_HEAD_V)),
        "gla_w_out": nrm(ks[12], (L, GLA_DV, D_MODEL), GLA_DV ** -0.5),
        "w_o": nrm(ks[13], (L, D_MODEL, D_MODEL), D_MODEL ** -0.5),
        "norm_ffn_g": gain(ks[14], (L, D_MODEL)),
        "ffn_w_up": nrm(ks[15], (L, D_MODEL, D_FF), D_MODEL ** -0.5),
        "ffn_w_down": nrm(ks[16], (L, D_FF, D_MODEL), D_FF ** -0.5),
        "ple_norm_g": gain(ks[17], (L, D_MODEL)),
        "ple_w_gate": nrm(ks[18], (L, D_MODEL, D_MODEL), D_MODEL ** -0.5),
        "ple_w_proj": nrm(ks[19], (L, PLE_DIM, D_MODEL), PLE_DIM ** -0.5),
        "final_norm_g": gain(ks[20], (D_MODEL,)),
    }


def reference(x, p, norm_mix_g, w_in, sg_ln_g, sg_ln_b, sg_w_s, sg_b_s, sg_w_out,
              gla_w_gate_up, gla_b_gate, gla_norm_g, gla_w_out, w_o,
              norm_ffn_g, ffn_w_up, ffn_w_down, ple_norm_g, ple_w_gate, ple_w_proj,
              final_norm_g):
    B, S, _ = x.shape
    for i in range(DEPTH):
        h = rms_norm(x, norm_mix_g[i])
        proj = jnp.einsum('bsd,dc->bsc', h, w_in[i])
        sg_u, sg_v, q, k, v, gate_lr, out_gate, bg_a, bg_b = split_cols(proj)

        sg_u = jax.nn.gelu(sg_u)
        sg_v = layer_norm(jax.nn.gelu(sg_v), sg_ln_g[i], sg_ln_b[i])
        z = spatial_gating(sg_u, sg_v, sg_w_s[i], sg_b_s[i])
        y_a = jnp.einsum('bsc,cd->bsd', z, sg_w_out[i])

        log_a = jax.nn.log_sigmoid(
            (jnp.einsum('bsr,rk->bsk', gate_lr, gla_w_gate_up[i]) + gla_b_gate[i]).astype(jnp.float32)
        ) / GLA_GATE_TEMP
        qh = q.reshape(B, S, GLA_HEADS, GLA_HEAD_K) * (GLA_HEAD_K ** -0.5)
        kh = k.reshape(B, S, GLA_HEADS, GLA_HEAD_K)
        vh = v.reshape(B, S, GLA_HEADS, GLA_HEAD_V)
        ah = log_a.reshape(B, S, GLA_HEADS, GLA_HEAD_K)
        o = gla_chunked(qh, kh, vh, ah)
        o = rms_norm(o, gla_norm_g[i]).astype(x.dtype).reshape(B, S, GLA_DV)
        o = o * jax.nn.silu(out_gate)
        y_b = jnp.einsum('bsc,cd->bsd', o, gla_w_out[i])

        merged = jax.nn.sigmoid(bg_a) * y_a + jax.nn.sigmoid(bg_b) * y_b
        x = x + jnp.einsum('bsd,de->bse', merged, w_o[i])

        h2 = rms_norm(x, norm_ffn_g[i])
        up = jnp.square(jax.nn.relu(jnp.einsum('bsd,df->bsf', h2, ffn_w_up[i])))
        x = x + jnp.einsum('bsf,fd->bsd', up, ffn_w_down[i])

        h3 = rms_norm(x, ple_norm_g[i])
        gate = jax.nn.sigmoid(jnp.einsum('bsd,de->bse', h3, ple_w_gate[i]))
        ple = jnp.einsum('bsp,pd->bsd', p[i].astype(x.dtype), ple_w_proj[i])
        x = x + gate * ple
    return rms_norm(x, final_norm_g)
```

```python
import functools

import jax
import jax.numpy as jnp
from jax import lax
from jax.experimental import pallas as pl
from jax.experimental.pallas import tpu as pltpu

D_MODEL = 1024
PLE_DIM = 256
SG_GROUPS = 4
SG_CHUNK = 128
SG_WIDTH = D_MODEL // 2
SG_GROUP_DIM = SG_WIDTH // SG_GROUPS
GLA_HEADS = 4
GLA_DK = D_MODEL // 2
GLA_DV = D_MODEL
GLA_HEAD_K = GLA_DK // GLA_HEADS
GLA_HEAD_V = GLA_DV // GLA_HEADS
GLA_GATE_RANK = 16
GLA_GATE_TEMP = 16.0
GLA_CHUNK = 64
D_FF = 4 * D_MODEL
EPS = 1e-6

_C_U = 0
_C_V = _C_U + SG_WIDTH
_C_Q = _C_V + SG_WIDTH
_C_K = _C_Q + GLA_DK
_C_GV = _C_K + GLA_DK
_C_OG = _C_GV + GLA_DV
_C_BA = _C_OG + GLA_DV
_C_BB = _C_BA + D_MODEL
_C_END = _C_BB + D_MODEL

MIX_TILE = 256
FFN_TILE = 512
FFN_SLAB = 1024
VMEM_LIMIT_BYTES = 56 * 1024 * 1024

_BF = jnp.bfloat16
_F32 = jnp.float32


def _dot(a, b):
    return jnp.dot(a, b, preferred_element_type=_F32)


def _rms(x, g):
    ms = jnp.mean(x * x, axis=-1, keepdims=True)
    return x * lax.rsqrt(ms + EPS) * g


def _gelu_tanh(x):
    c = 0.7978845608028654
    return x * (0.5 * (1.0 + jnp.tanh(c * (x + 0.044715 * (x * x * x)))))


def _log_sigmoid(x):
    return jnp.minimum(x, 0.0) - jnp.log1p(jnp.exp(-jnp.abs(x)))


def _mix_kernel(x_ref, g_ref, w_main_ref, w_glr_ref, ln_g_ref, ln_b_ref, ws_ref, bs_ref,
                sg_wout_ref, wgu_ref, bgate_ref, gn_g_ref, gla_wout_ref, wo_ref,
                out_ref, state_ref, o_scr):
    ts = x_ref.shape[0]

    @pl.when(pl.program_id(1) == 0)
    def _():
        state_ref[...] = jnp.zeros_like(state_ref)

    x = x_ref[...]
    hb = _rms(x, g_ref[...]).astype(_BF)

    uv = _dot(hb, w_main_ref[:, _C_U:_C_Q])
    u = _gelu_tanh(uv[:, :SG_WIDTH])
    v = _gelu_tanh(uv[:, SG_WIDTH:])
    mu = jnp.mean(v, axis=-1, keepdims=True)
    vc = v - mu
    var = jnp.mean(vc * vc, axis=-1, keepdims=True)
    vn = (vc * lax.rsqrt(var + EPS) * ln_g_ref[...] + ln_b_ref[...]).astype(_BF)

    row = lax.broadcasted_iota(jnp.int32, (SG_CHUNK, SG_CHUNK), 0)
    col = lax.broadcasted_iota(jnp.int32, (SG_CHUNK, SG_CHUNK), 1)
    tril = row >= col
    w_sp = [jnp.where(tril, ws_ref[g], 0.0).astype(_BF) for g in range(SG_GROUPS)]
    bias = bs_ref[...]
    mixed_rows = []
    for c in range(ts // SG_CHUNK):
        r0 = c * SG_CHUNK
        blocks = [
            _dot(w_sp[g], vn[r0:r0 + SG_CHUNK, g * SG_GROUP_DIM:(g + 1) * SG_GROUP_DIM])
            for g in range(SG_GROUPS)
        ]
        mixed_rows.append(jnp.concatenate(blocks, axis=1) + bias)
    mixed = jnp.concatenate(mixed_rows, axis=0)
    z = (u * mixed).astype(_BF)
    y_a = _dot(z, sg_wout_ref[...])

    qkv = _dot(hb, w_main_ref[:, _C_Q:_C_OG])
    q = qkv[:, :GLA_DK] * (GLA_HEAD_K ** -0.5)
    k = qkv[:, GLA_DK:2 * GLA_DK]
    vb = qkv[:, 2 * GLA_DK:].astype(_BF)
    glr = _dot(hb, w_glr_ref[...])
    zg = _dot(glr.astype(_BF), wgu_ref[...]) + bgate_ref[...]
    log_a = _log_sigmoid(zg) * (1.0 / GLA_GATE_TEMP)

    ri = lax.broadcasted_iota(jnp.int32, (ts, ts), 0)
    ci = lax.broadcasted_iota(jnp.int32, (ts, ts), 1)
    lmat = jnp.where((ri // GLA_CHUNK == ci // GLA_CHUNK) & (ci <= ri), 1.0, 0.0).astype(_BF)
    la_hi = log_a.astype(_BF)
    la_lo = (log_a - la_hi.astype(_F32)).astype(_BF)
    b = _dot(lmat, la_hi) + _dot(lmat, la_lo)

    r64 = lax.broadcasted_iota(jnp.int32, (GLA_CHUNK, GLA_CHUNK), 0)
    c64 = lax.broadcasted_iota(jnp.int32, (GLA_CHUNK, GLA_CHUNK), 1)
    causal = r64 >= c64
    rk = lax.broadcasted_iota(jnp.int32, (GLA_HEAD_K, GLA_HEAD_K), 0)
    ck = lax.broadcasted_iota(jnp.int32, (GLA_HEAD_K, GLA_HEAD_K), 1)
    eye_k = rk == ck

    for c in range(ts // GLA_CHUNK):
        r0 = c * GLA_CHUNK
        bc = b[r0:r0 + GLA_CHUNK]
        b_mid = bc[GLA_CHUNK // 2 - 1:GLA_CHUNK // 2]
        b_last = bc[GLA_CHUNK - 1:GLA_CHUNK]
        qc = q[r0:r0 + GLA_CHUNK]
        kc = k[r0:r0 + GLA_CHUNK]
        q_in = (qc * jnp.exp(bc - b_mid)).astype(_BF)
        k_in = (kc * jnp.exp(b_mid - bc)).astype(_BF)
        k_dec = (kc * jnp.exp(b_last - bc)).astype(_BF)
        q_dec = (qc * jnp.exp(bc)).astype(_BF)
        dec = jnp.exp(b_last)
        for hd in range(GLA_HEADS):
            ks = slice(hd * GLA_HEAD_K, (hd + 1) * GLA_HEAD_K)
            vs = slice(hd * GLA_HEAD_V, (hd + 1) * GLA_HEAD_V)
            scores = lax.dot_general(q_in[:, ks], k_in[:, ks], (((1,), (1,)), ((), ())),
                                     preferred_element_type=_F32)
            scores = jnp.where(causal, scores, 0.0).astype(_BF)
            vh = vb[r0:r0 + GLA_CHUNK, vs]
            s_prev = state_ref[hd]
            o_scr[r0:r0 + GLA_CHUNK, vs] = _dot(scores, vh) + _dot(q_dec[:, ks], s_prev.astype(_BF))
            kv = lax.dot_general(k_dec[:, ks], vh, (((0,), (0,)), ((), ())),
                                 preferred_element_type=_F32)
            dec_col = jnp.sum(jnp.where(eye_k, jnp.broadcast_to(dec[:, ks], (GLA_HEAD_K, GLA_HEAD_K)), 0.0),
                              axis=1, keepdims=True)
            state_ref[hd] = dec_col * s_prev + kv

    og = _dot(hb, w_main_ref[:, _C_OG:_C_BA])
    gn_g = gn_g_ref[...]
    o_heads = []
    for hd in range(GLA_HEADS):
        oh = o_scr[:, hd * GLA_HEAD_V:(hd + 1) * GLA_HEAD_V]
        o_heads.append(_rms(oh, gn_g))
    o_n = jnp.concatenate(o_heads, axis=1)
    o_g = (o_n * (og * jax.nn.sigmoid(og))).astype(_BF)
    y_b = _dot(o_g, gla_wout_ref[...])

    bg = _dot(hb, w_main_ref[:, _C_BA:_C_END])
    merged = jax.nn.sigmoid(bg[:, :D_MODEL]) * y_a + jax.nn.sigmoid(bg[:, D_MODEL:]) * y_b
    out_ref[...] = x + _dot(merged.astype(_BF), wo_ref[...])


def _ffn_kernel(x_ref, p_ref, g_ffn_ref, w_up_ref, w_down_ref, g_ple_ref, w_pg_ref, w_pp_ref,
                g_fin_ref, out_ref):
    x = x_ref[...]
    hb = _rms(x, g_ffn_ref[...]).astype(_BF)
    acc = x
    for f0 in range(0, D_FF, FFN_SLAB):
        up = jnp.maximum(_dot(hb, w_up_ref[:, f0:f0 + FFN_SLAB]), 0.0)
        acc = acc + _dot((up * up).astype(_BF), w_down_ref[f0:f0 + FFN_SLAB, :])
    h3 = _rms(acc, g_ple_ref[...]).astype(_BF)
    gate = jax.nn.sigmoid(_dot(h3, w_pg_ref[...]))
    ple = _dot(p_ref[...].astype(_BF), w_pp_ref[...])
    out_ref[...] = _rms(acc + gate * ple, g_fin_ref[...])


def _resident(shape):
    zeros = (0,) * len(shape)
    return pl.BlockSpec(shape, lambda *_: zeros, pipeline_mode=pl.Buffered(1))


def kernel(x, p, norm_mix_g, w_in, sg_ln_g, sg_ln_b, sg_w_s, sg_b_s, sg_w_out, gla_w_gate_up, gla_b_gate, gla_norm_g, gla_w_out, w_o, norm_ffn_g, ffn_w_up, ffn_w_down, ple_norm_g, ple_w_gate, ple_w_proj, final_norm_g):
    batch, seq, d = x.shape
    assert w_in.shape[0] == 1
    assert d == D_MODEL and seq % MIX_TILE == 0 and (batch * seq) % FFN_TILE == 0
    tokens = batch * seq

    for i in range(1):
        glr0 = _C_OG
        w_main = jnp.concatenate([w_in[i][:, :glr0], w_in[i][:, glr0 + GLA_GATE_RANK:]], axis=1).astype(_BF)
        w_glr = w_in[i][:, glr0:glr0 + GLA_GATE_RANK].astype(_BF)
        bias_sp = jnp.repeat(sg_b_s[i].T, SG_GROUP_DIM, axis=1)

        mix_inputs = (
            x,
            norm_mix_g[i].reshape(1, D_MODEL),
            w_main,
            w_glr,
            sg_ln_g[i].reshape(1, SG_WIDTH),
            sg_ln_b[i].reshape(1, SG_WIDTH),
            sg_w_s[i],
            bias_sp,
            sg_w_out[i].astype(_BF),
            gla_w_gate_up[i].astype(_BF),
            gla_b_gate[i].reshape(1, GLA_DK),
            gla_norm_g[i].reshape(1, GLA_HEAD_V),
            gla_w_out[i].astype(_BF),
            w_o[i].astype(_BF),
        )
        x_spec = pl.BlockSpec((None, MIX_TILE, D_MODEL), lambda bi, si: (bi, si, 0))
        x = pl.pallas_call(
            _mix_kernel,
            grid=(batch, seq // MIX_TILE),
            in_specs=[x_spec] + [_resident(a.shape) for a in mix_inputs[1:]],
            out_specs=x_spec,
            out_shape=jax.ShapeDtypeStruct((batch, seq, D_MODEL), _F32),
            scratch_shapes=[
                pltpu.VMEM((GLA_HEADS, GLA_HEAD_K, GLA_HEAD_V), _F32),
                pltpu.VMEM((MIX_TILE, GLA_DV), _F32),
            ],
            compiler_params=pltpu.CompilerParams(
                dimension_semantics=("arbitrary", "arbitrary"),
                vmem_limit_bytes=VMEM_LIMIT_BYTES,
            ),
            name="token_mixing",
        )(*mix_inputs)

        ffn_inputs = (
            x.reshape(tokens, D_MODEL),
            p[i].reshape(tokens, PLE_DIM),
            norm_ffn_g[i].reshape(1, D_MODEL),
            ffn_w_up[i].astype(_BF),
            ffn_w_down[i].astype(_BF),
            ple_norm_g[i].reshape(1, D_MODEL),
            ple_w_gate[i].astype(_BF),
            ple_w_proj[i].astype(_BF),
            final_norm_g.reshape(1, D_MODEL),
        )
        tok_spec = pl.BlockSpec((FFN_TILE, D_MODEL), lambda t: (t, 0))
        x = pl.pallas_call(
            _ffn_kernel,
            grid=(tokens // FFN_TILE,),
            in_specs=[tok_spec, pl.BlockSpec((FFN_TILE, PLE_DIM), lambda t: (t, 0))]
            + [_resident(a.shape) for a in ffn_inputs[2:]],
            out_specs=tok_spec,
            out_shape=jax.ShapeDtypeStruct((tokens, D_MODEL), _F32),
            compiler_params=pltpu.CompilerParams(
                dimension_semantics=("arbitrary",),
                vmem_limit_bytes=VMEM_LIMIT_BYTES,
            ),
            name="channel_mixing",
        )(*ffn_inputs).reshape(batch, seq, D_MODEL)
    return x
```

```python
import functools

import jax
import jax.numpy as jnp
from jax import lax
from jax.experimental import pallas as pl
from jax.experimental.pallas import tpu as pltpu

D_MODEL = 1024
PLE_DIM = 256
SG_GROUPS = 4
SG_CHUNK = 128
SG_WIDTH = D_MODEL // 2
SG_GROUP_DIM = SG_WIDTH // SG_GROUPS
GLA_HEADS = 4
GLA_DK = D_MODEL // 2
GLA_DV = D_MODEL
GLA_HEAD_K = GLA_DK // GLA_HEADS
GLA_HEAD_V = GLA_DV // GLA_HEADS
GLA_GATE_RANK = 16
GLA_GATE_TEMP = 16.0
GLA_CHUNK = 64
D_FF = 4 * D_MODEL
EPS = 1e-6

_C_U = 0
_C_Q = _C_U + 2 * SG_WIDTH
_C_LR = _C_Q + 2 * GLA_DK + GLA_DV
_C_OG = _C_LR + GLA_GATE_RANK
_T_BA = GLA_DV
_T_END = _T_BA + 2 * D_MODEL

MIX_TILE = 256
FFN_TILE = 512
FFN_SLAB = 1024
VMEM_LIMIT_BYTES = 56 * 1024 * 1024

_BF = jnp.bfloat16
_F32 = jnp.float32

_dot = functools.partial(jnp.dot, preferred_element_type=_F32)


def _rms(x, g):
    ms = jnp.mean(x * x, axis=-1, keepdims=True)
    return x * lax.rsqrt(ms + EPS) * g


def _gelu_tanh(x):
    c = 0.7978845608028654
    return x * (0.5 * (1.0 + jnp.tanh(c * (x + 0.044715 * (x * x * x)))))


def _sigmoid(x):
    return 0.5 * jnp.tanh(0.5 * x) + 0.5


def _log_sigmoid(x):
    return jnp.minimum(x, 0.0) - jnp.log1p(jnp.exp(-jnp.abs(x)))


def _mix_kernel(x_ref, g_ref, lmat_ref, w_head_ref, w_tail_ref, w_glr_ref, ln_g_ref, ln_b_ref, ws_ref, bs_ref,
                sg_wout_ref, wgu_ref, bgate_ref, gn_g_ref, gla_wout_ref, wo_ref,
                out_ref, state_ref, o_scr):
    ts = x_ref.shape[0]

    @pl.when(pl.program_id(1) == 0)
    def _():
        state_ref[...] = jnp.zeros_like(state_ref)

    x = x_ref[...]
    hb = _rms(x, g_ref[...]).astype(_BF)

    glr = _dot(hb, w_glr_ref[...])
    uv = _dot(hb, w_head_ref[:, _C_U:_C_Q])
    zg = _dot(glr.astype(_BF), wgu_ref[...]) + bgate_ref[...]
    qkv = _dot(hb, w_head_ref[:, _C_Q:_C_LR])

    log_a = _log_sigmoid(zg) * (1.0 / GLA_GATE_TEMP)
    la_hi = log_a.astype(_BF)
    la_lo = (log_a - la_hi.astype(_F32)).astype(_BF)
    lmat = lmat_ref[...]
    b = _dot(lmat, la_hi) + _dot(lmat, la_lo)

    u = _gelu_tanh(uv[:, :SG_WIDTH])
    v = _gelu_tanh(uv[:, SG_WIDTH:])
    mu = jnp.mean(v, axis=-1, keepdims=True)
    vc = v - mu
    var = jnp.mean(vc * vc, axis=-1, keepdims=True)
    vn = (vc * lax.rsqrt(var + EPS) * ln_g_ref[...] + ln_b_ref[...]).astype(_BF)

    og = _dot(hb, w_tail_ref[:, :_T_BA])

    q = qkv[:, :GLA_DK] * (GLA_HEAD_K ** -0.5)
    k = qkv[:, GLA_DK:2 * GLA_DK]
    vb = qkv[:, 2 * GLA_DK:].astype(_BF)
    r64 = lax.broadcasted_iota(jnp.int32, (GLA_CHUNK, GLA_CHUNK), 0)
    c64 = lax.broadcasted_iota(jnp.int32, (GLA_CHUNK, GLA_CHUNK), 1)
    causal = r64 >= c64
    rk = lax.broadcasted_iota(jnp.int32, (GLA_HEAD_K, GLA_HEAD_K), 0)
    ck = lax.broadcasted_iota(jnp.int32, (GLA_HEAD_K, GLA_HEAD_K), 1)
    eye_k = rk == ck
    n_chunks = ts // GLA_CHUNK
    scores_l, kv_l, qdec_l, dcol_l = {}, {}, {}, {}
    for c in range(n_chunks):
        r0 = c * GLA_CHUNK
        bc = b[r0:r0 + GLA_CHUNK]
        b_mid = bc[GLA_CHUNK // 2 - 1:GLA_CHUNK // 2]
        b_last = bc[GLA_CHUNK - 1:GLA_CHUNK]
        qc = q[r0:r0 + GLA_CHUNK]
        kc = k[r0:r0 + GLA_CHUNK]
        q_in = (qc * jnp.exp(bc - b_mid)).astype(_BF)
        k_in = (kc * jnp.exp(b_mid - bc)).astype(_BF)
        k_dec = (kc * jnp.exp(b_last - bc)).astype(_BF)
        qdec_l[c] = (qc * jnp.exp(bc)).astype(_BF)
        dec = jnp.exp(b_last)
        for hd in range(GLA_HEADS):
            ks = slice(hd * GLA_HEAD_K, (hd + 1) * GLA_HEAD_K)
            vs = slice(hd * GLA_HEAD_V, (hd + 1) * GLA_HEAD_V)
            sc = lax.dot_general(q_in[:, ks], k_in[:, ks], (((1,), (1,)), ((), ())),
                                 preferred_element_type=_F32)
            scores_l[c, hd] = jnp.where(causal, sc, 0.0).astype(_BF)
            kv_l[c, hd] = lax.dot_general(k_dec[:, ks], vb[r0:r0 + GLA_CHUNK, vs], (((0,), (0,)), ((), ())),
                                          preferred_element_type=_F32)
            dcol_l[c, hd] = jnp.sum(
                jnp.where(eye_k, jnp.broadcast_to(dec[:, ks], (GLA_HEAD_K, GLA_HEAD_K)), 0.0),
                axis=1, keepdims=True)

    bg = _dot(hb, w_tail_ref[:, _T_BA:_T_END])

    row = lax.broadcasted_iota(jnp.int32, (SG_CHUNK, SG_CHUNK), 0)
    col = lax.broadcasted_iota(jnp.int32, (SG_CHUNK, SG_CHUNK), 1)
    tril = row >= col
    w_sp = [jnp.where(tril, ws_ref[g], 0.0).astype(_BF) for g in range(SG_GROUPS)]
    bias = bs_ref[...]
    mixed_rows = []
    for c in range(ts // SG_CHUNK):
        r0 = c * SG_CHUNK
        blocks = [
            _dot(w_sp[g], vn[r0:r0 + SG_CHUNK, g * SG_GROUP_DIM:(g + 1) * SG_GROUP_DIM])
            for g in range(SG_GROUPS)
        ]
        mixed_rows.append(jnp.concatenate(blocks, axis=1) + bias)
    mixed = jnp.concatenate(mixed_rows, axis=0)
    z = (u * mixed).astype(_BF)
    y_a = _dot(z, sg_wout_ref[...])

    sprev_l = {}
    for hd in range(GLA_HEADS):
        s_run = state_ref[hd]
        for c in range(n_chunks):
            sprev_l[c, hd] = s_run.astype(_BF)
            s_run = dcol_l[c, hd] * s_run + kv_l[c, hd]
        state_ref[hd] = s_run

    for c in range(n_chunks):
        r0 = c * GLA_CHUNK
        for hd in range(GLA_HEADS):
            ks = slice(hd * GLA_HEAD_K, (hd + 1) * GLA_HEAD_K)
            vs = slice(hd * GLA_HEAD_V, (hd + 1) * GLA_HEAD_V)
            o_scr[r0:r0 + GLA_CHUNK, vs] = (_dot(scores_l[c, hd], vb[r0:r0 + GLA_CHUNK, vs])
                                            + _dot(qdec_l[c][:, ks], sprev_l[c, hd]))
    gn_g = gn_g_ref[...]
    o_heads = []
    for hd in range(GLA_HEADS):
        oh = o_scr[:, hd * GLA_HEAD_V:(hd + 1) * GLA_HEAD_V]
        o_heads.append(_rms(oh, gn_g))
    o_n = jnp.concatenate(o_heads, axis=1)
    o_g = (o_n * (og * _sigmoid(og))).astype(_BF)
    y_b = _dot(o_g, gla_wout_ref[...])

    merged = _sigmoid(bg[:, :D_MODEL]) * y_a + _sigmoid(bg[:, D_MODEL:]) * y_b
    out_ref[...] = x + _dot(merged.astype(_BF), wo_ref[...])


def _ffn_kernel(x_ref, p_ref, g_ffn_ref, w_up_ref, w_down_ref, g_ple_ref, w_pg_ref, w_pp_ref,
                g_fin_ref, out_ref):
    x = x_ref[...]
    hb = _rms(x, g_ffn_ref[...]).astype(_BF)
    acc = x
    for f0 in range(0, D_FF, FFN_SLAB):
        up = jnp.maximum(_dot(hb, w_up_ref[:, f0:f0 + FFN_SLAB]), 0.0)
        acc = acc + _dot((up * up).astype(_BF), w_down_ref[f0:f0 + FFN_SLAB, :])
    h3 = _rms(acc, g_ple_ref[...]).astype(_BF)
    gate = _sigmoid(_dot(h3, w_pg_ref[...]))
    ple = _dot(p_ref[...].astype(_BF), w_pp_ref[...])
    out_ref[...] = _rms(acc + gate * ple, g_fin_ref[...])


def _resident(shape):
    zeros = (0,) * len(shape)
    return pl.BlockSpec(shape, lambda *_: zeros, pipeline_mode=pl.Buffered(1))


def kernel(x, p, norm_mix_g, w_in, sg_ln_g, sg_ln_b, sg_w_s, sg_b_s, sg_w_out, gla_w_gate_up, gla_b_gate, gla_norm_g, gla_w_out, w_o, norm_ffn_g, ffn_w_up, ffn_w_down, ple_norm_g, ple_w_gate, ple_w_proj, final_norm_g):
    batch, seq, d = x.shape
    assert w_in.shape[0] == 1
    assert d == D_MODEL and seq % MIX_TILE == 0 and (batch * seq) % FFN_TILE == 0
    tokens = batch * seq

    for i in range(1):
        w_head = w_in[i][:, :_C_LR].astype(_BF)
        w_glr = w_in[i][:, _C_LR:_C_OG].astype(_BF)
        w_tail = w_in[i][:, _C_OG:].astype(_BF)
        bias_sp = jnp.repeat(sg_b_s[i].T, SG_GROUP_DIM, axis=1)
        pos = jnp.arange(MIX_TILE)
        lmat = ((pos[:, None] // GLA_CHUNK == pos[None, :] // GLA_CHUNK) & (pos[None, :] <= pos[:, None])).astype(_BF)

        mix_inputs = (
            x,
            norm_mix_g[i].reshape(1, D_MODEL),
            lmat,
            w_head,
            w_tail,
            w_glr,
            sg_ln_g[i].reshape(1, SG_WIDTH),
            sg_ln_b[i].reshape(1, SG_WIDTH),
            sg_w_s[i],
            bias_sp,
            sg_w_out[i].astype(_BF),
            gla_w_gate_up[i].astype(_BF),
            gla_b_gate[i].reshape(1, GLA_DK),
            gla_norm_g[i].reshape(1, GLA_HEAD_V),
            gla_w_out[i].astype(_BF),
            w_o[i].astype(_BF),
        )
        x_spec = pl.BlockSpec((None, MIX_TILE, D_MODEL), lambda bi, si: (bi, si, 0))
        x = pl.pallas_call(
            _mix_kernel,
            grid=(batch, seq // MIX_TILE),
            in_specs=[x_spec] + [_resident(a.shape) for a in mix_inputs[1:]],
            out_specs=x_spec,
            out_shape=jax.ShapeDtypeStruct((batch, seq, D_MODEL), _F32),
            scratch_shapes=[
                pltpu.VMEM((GLA_HEADS, GLA_HEAD_K, GLA_HEAD_V), _F32),
                pltpu.VMEM((MIX_TILE, GLA_DV), _F32),
            ],
            compiler_params=pltpu.CompilerParams(
                dimension_semantics=("arbitrary", "arbitrary"),
                vmem_limit_bytes=VMEM_LIMIT_BYTES,
            ),
            name="token_mixing",
        )(*mix_inputs)

        ffn_inputs = (
            x.reshape(tokens, D_MODEL),
            p[i].reshape(tokens, PLE_DIM),
            norm_ffn_g[i].reshape(1, D_MODEL),
            ffn_w_up[i].astype(_BF),
            ffn_w_down[i].astype(_BF),
            ple_norm_g[i].reshape(1, D_MODEL),
            ple_w_gate[i].astype(_BF),
            ple_w_proj[i].astype(_BF),
            final_norm_g.reshape(1, D_MODEL),
        )
        tok_spec = pl.BlockSpec((FFN_TILE, D_MODEL), lambda t: (t, 0))
        x = pl.pallas_call(
            _ffn_kernel,
            grid=(tokens // FFN_TILE,),
            in_specs=[tok_spec, pl.BlockSpec((FFN_TILE, PLE_DIM), lambda t: (t, 0))]
            + [_resident(a.shape) for a in ffn_inputs[2:]],
            out_specs=tok_spec,
            out_shape=jax.ShapeDtypeStruct((tokens, D_MODEL), _F32),
            compiler_params=pltpu.CompilerParams(
                dimension_semantics=("arbitrary",),
                vmem_limit_bytes=VMEM_LIMIT_BYTES,
            ),
            name="channel_mixing",
        )(*ffn_inputs).reshape(batch, seq, D_MODEL)
    return x
```

```python
import functools

import jax
import jax.numpy as jnp
from jax import lax
from jax.experimental import pallas as pl
from jax.experimental.pallas import tpu as pltpu

D_MODEL = 1024
PLE_DIM = 256
SG_GROUPS = 4
SG_CHUNK = 128
SG_WIDTH = D_MODEL // 2
SG_GROUP_DIM = SG_WIDTH // SG_GROUPS
GLA_HEADS = 4
GLA_DK = D_MODEL // 2
GLA_DV = D_MODEL
GLA_HEAD_K = GLA_DK // GLA_HEADS
GLA_HEAD_V = GLA_DV // GLA_HEADS
GLA_GATE_RANK = 16
GLA_GATE_TEMP = 16.0
GLA_CHUNK = 64
D_FF = 4 * D_MODEL
EPS = 1e-6

_C_U = 0
_C_Q = _C_U + 2 * SG_WIDTH
_C_LR = _C_Q + 2 * GLA_DK + GLA_DV
_C_OG = _C_LR + GLA_GATE_RANK
_T_BA = GLA_DV
_T_END = _T_BA + 2 * D_MODEL

MIX_TILE = 512
CUMSUM_ROWS = 256
FFN_TILE = 512
FFN_SLAB = 1024
VMEM_LIMIT_BYTES = 56 * 1024 * 1024

_BF = jnp.bfloat16
_F32 = jnp.float32

_dot = functools.partial(jnp.dot, preferred_element_type=_F32)


def _rms(x, g):
    ms = jnp.mean(x * x, axis=-1, keepdims=True)
    return x * lax.rsqrt(ms + EPS) * g


def _gelu_tanh(x):
    c = 0.7978845608028654
    return x * (0.5 * (1.0 + jnp.tanh(c * (x + 0.044715 * (x * x * x)))))


def _sigmoid(x):
    return 0.5 * jnp.tanh(0.5 * x) + 0.5


def _log_sigmoid(x):
    return jnp.minimum(x, 0.0) - jnp.log1p(jnp.exp(-jnp.abs(x)))


def _mix_kernel(x_ref, g_ref, lmat_ref, w_head_ref, w_tail_ref, w_glr_ref, ln_g_ref, ln_b_ref, ws_ref, bs_ref,
                sg_wout_ref, wgu_ref, bgate_ref, gn_g_ref, gla_wout_ref, wo_ref,
                out_ref, state_ref, o_scr):
    ts = x_ref.shape[0]

    @pl.when(pl.program_id(1) == 0)
    def _():
        state_ref[...] = jnp.zeros_like(state_ref)

    x = x_ref[...]
    hb = _rms(x, g_ref[...]).astype(_BF)

    glr = _dot(hb, w_glr_ref[...])
    uv = _dot(hb, w_head_ref[:, _C_U:_C_Q])
    zg = _dot(glr.astype(_BF), wgu_ref[...]) + bgate_ref[...]
    qkv = _dot(hb, w_head_ref[:, _C_Q:_C_LR])

    log_a = _log_sigmoid(zg) * (1.0 / GLA_GATE_TEMP)
    la_hi = log_a.astype(_BF)
    la_lo = (log_a - la_hi.astype(_F32)).astype(_BF)
    lmat = lmat_ref[...]
    b = jnp.concatenate(
        [_dot(lmat, la_hi[r0:r0 + CUMSUM_ROWS]) + _dot(lmat, la_lo[r0:r0 + CUMSUM_ROWS])
         for r0 in range(0, ts, CUMSUM_ROWS)], axis=0)

    u = _gelu_tanh(uv[:, :SG_WIDTH])
    v = _gelu_tanh(uv[:, SG_WIDTH:])
    mu = jnp.mean(v, axis=-1, keepdims=True)
    vc = v - mu
    var = jnp.mean(vc * vc, axis=-1, keepdims=True)
    vn = (vc * lax.rsqrt(var + EPS) * ln_g_ref[...] + ln_b_ref[...]).astype(_BF)

    og = _dot(hb, w_tail_ref[:, :_T_BA])

    q = qkv[:, :GLA_DK] * (GLA_HEAD_K ** -0.5)
    k = qkv[:, GLA_DK:2 * GLA_DK]
    vb = qkv[:, 2 * GLA_DK:].astype(_BF)
    r64 = lax.broadcasted_iota(jnp.int32, (GLA_CHUNK, GLA_CHUNK), 0)
    c64 = lax.broadcasted_iota(jnp.int32, (GLA_CHUNK, GLA_CHUNK), 1)
    causal = r64 >= c64
    rk = lax.broadcasted_iota(jnp.int32, (GLA_HEAD_K, GLA_HEAD_K), 0)
    ck = lax.broadcasted_iota(jnp.int32, (GLA_HEAD_K, GLA_HEAD_K), 1)
    eye_k = rk == ck
    n_chunks = ts // GLA_CHUNK
    scores_l, kv_l, qdec_l, dcol_l = {}, {}, {}, {}
    for c in range(n_chunks):
        r0 = c * GLA_CHUNK
        bc = b[r0:r0 + GLA_CHUNK]
        b_mid = bc[GLA_CHUNK // 2 - 1:GLA_CHUNK // 2]
        b_last = bc[GLA_CHUNK - 1:GLA_CHUNK]
        qc = q[r0:r0 + GLA_CHUNK]
        kc = k[r0:r0 + GLA_CHUNK]
        q_in = (qc * jnp.exp(bc - b_mid)).astype(_BF)
        k_in = (kc * jnp.exp(b_mid - bc)).astype(_BF)
        k_dec = (kc * jnp.exp(b_last - bc)).astype(_BF)
        qdec_l[c] = (qc * jnp.exp(bc)).astype(_BF)
        dec = jnp.exp(b_last)
        for hd in range(GLA_HEADS):
            ks = slice(hd * GLA_HEAD_K, (hd + 1) * GLA_HEAD_K)
            vs = slice(hd * GLA_HEAD_V, (hd + 1) * GLA_HEAD_V)
            sc = lax.dot_general(q_in[:, ks], k_in[:, ks], (((1,), (1,)), ((), ())),
                                 preferred_element_type=_F32)
            scores_l[c, hd] = jnp.where(causal, sc, 0.0).astype(_BF)
            kv_l[c, hd] = lax.dot_general(k_dec[:, ks], vb[r0:r0 + GLA_CHUNK, vs], (((0,), (0,)), ((), ())),
                                          preferred_element_type=_F32)
            dcol_l[c, hd] = jnp.sum(
                jnp.where(eye_k, jnp.broadcast_to(dec[:, ks], (GLA_HEAD_K, GLA_HEAD_K)), 0.0),
                axis=1, keepdims=True)

    bg = _dot(hb, w_tail_ref[:, _T_BA:_T_END])

    row = lax.broadcasted_iota(jnp.int32, (SG_CHUNK, SG_CHUNK), 0)
    col = lax.broadcasted_iota(jnp.int32, (SG_CHUNK, SG_CHUNK), 1)
    tril = row >= col
    w_sp = [jnp.where(tril, ws_ref[g], 0.0).astype(_BF) for g in range(SG_GROUPS)]
    bias = bs_ref[...]
    mixed_rows = []
    for c in range(ts // SG_CHUNK):
        r0 = c * SG_CHUNK
        blocks = [
            _dot(w_sp[g], vn[r0:r0 + SG_CHUNK, g * SG_GROUP_DIM:(g + 1) * SG_GROUP_DIM])
            for g in range(SG_GROUPS)
        ]
        mixed_rows.append(jnp.concatenate(blocks, axis=1) + bias)
    mixed = jnp.concatenate(mixed_rows, axis=0)
    z = (u * mixed).astype(_BF)
    y_a = _dot(z, sg_wout_ref[...])

    sprev_l = {}
    for hd in range(GLA_HEADS):
        s_run = state_ref[hd]
        for c in range(n_chunks):
            sprev_l[c, hd] = s_run.astype(_BF)
            s_run = dcol_l[c, hd] * s_run + kv_l[c, hd]
        state_ref[hd] = s_run

    for c in range(n_chunks):
        r0 = c * GLA_CHUNK
        for hd in range(GLA_HEADS):
            ks = slice(hd * GLA_HEAD_K, (hd + 1) * GLA_HEAD_K)
            vs = slice(hd * GLA_HEAD_V, (hd + 1) * GLA_HEAD_V)
            o_scr[r0:r0 + GLA_CHUNK, vs] = (_dot(scores_l[c, hd], vb[r0:r0 + GLA_CHUNK, vs])
                                            + _dot(qdec_l[c][:, ks], sprev_l[c, hd]))
    gn_g = gn_g_ref[...]
    o_heads = []
    for hd in range(GLA_HEADS):
        oh = o_scr[:, hd * GLA_HEAD_V:(hd + 1) * GLA_HEAD_V]
        o_heads.append(_rms(oh, gn_g))
    o_n = jnp.concatenate(o_heads, axis=1)
    o_g = (o_n * (og * _sigmoid(og))).astype(_BF)
    y_b = _dot(o_g, gla_wout_ref[...])

    merged = _sigmoid(bg[:, :D_MODEL]) * y_a + _sigmoid(bg[:, D_MODEL:]) * y_b
    out_ref[...] = x + _dot(merged.astype(_BF), wo_ref[...])


def _ffn_kernel(x_ref, p_ref, g_ffn_ref, w_up_ref, w_down_ref, g_ple_ref, w_pg_ref, w_pp_ref,
                g_fin_ref, out_ref):
    x = x_ref[...]
    hb = _rms(x, g_ffn_ref[...]).astype(_BF)
    ple = _dot(p_ref[...].astype(_BF), w_pp_ref[...])
    acc = x
    for f0 in range(0, D_FF, FFN_SLAB):
        up = jnp.maximum(_dot(hb, w_up_ref[:, f0:f0 + FFN_SLAB]), 0.0)
        acc = acc + _dot((up * up).astype(_BF), w_down_ref[f0:f0 + FFN_SLAB, :])
    half = x_ref.shape[0] // 2
    for r0 in (0, half):
        acc_h = acc[r0:r0 + half]
        h3 = _rms(acc_h, g_ple_ref[...]).astype(_BF)
        gate = _sigmoid(_dot(h3, w_pg_ref[...]))
        out_ref[r0:r0 + half, :] = _rms(acc_h + gate * ple[r0:r0 + half], g_fin_ref[...])


def _resident(shape):
    zeros = (0,) * len(shape)
    return pl.BlockSpec(shape, lambda *_: zeros, pipeline_mode=pl.Buffered(1))


def kernel(x, p, norm_mix_g, w_in, sg_ln_g, sg_ln_b, sg_w_s, sg_b_s, sg_w_out, gla_w_gate_up, gla_b_gate, gla_norm_g, gla_w_out, w_o, norm_ffn_g, ffn_w_up, ffn_w_down, ple_norm_g, ple_w_gate, ple_w_proj, final_norm_g):
    batch, seq, d = x.shape
    assert w_in.shape[0] == 1
    assert d == D_MODEL and seq % MIX_TILE == 0 and (batch * seq) % FFN_TILE == 0
    tokens = batch * seq

    for i in range(1):
        w_head = w_in[i][:, :_C_LR].astype(_BF)
        w_glr = w_in[i][:, _C_LR:_C_OG].astype(_BF)
        w_tail = w_in[i][:, _C_OG:].astype(_BF)
        bias_sp = jnp.repeat(sg_b_s[i].T, SG_GROUP_DIM, axis=1)
        pos = jnp.arange(CUMSUM_ROWS)
        lmat = ((pos[:, None] // GLA_CHUNK == pos[None, :] // GLA_CHUNK) & (pos[None, :] <= pos[:, None])).astype(_BF)

        mix_inputs = (
            x,
            norm_mix_g[i].reshape(1, D_MODEL),
            lmat,
            w_head,
            w_tail,
            w_glr,
            sg_ln_g[i].reshape(1, SG_WIDTH),
            sg_ln_b[i].reshape(1, SG_WIDTH),
            sg_w_s[i],
            bias_sp,
            sg_w_out[i].astype(_BF),
            gla_w_gate_up[i].astype(_BF),
            gla_b_gate[i].reshape(1, GLA_DK),
            gla_norm_g[i].reshape(1, GLA_HEAD_V),
            gla_w_out[i].astype(_BF),
            w_o[i].astype(_BF),
        )
        x_spec = pl.BlockSpec((None, MIX_TILE, D_MODEL), lambda bi, si: (bi, si, 0))
        x = pl.pallas_call(
            _mix_kernel,
            grid=(batch, seq // MIX_TILE),
            in_specs=[x_spec] + [_resident(a.shape) for a in mix_inputs[1:]],
            out_specs=x_spec,
            out_shape=jax.ShapeDtypeStruct((batch, seq, D_MODEL), _F32),
            scratch_shapes=[
                pltpu.VMEM((GLA_HEADS, GLA_HEAD_K, GLA_HEAD_V), _F32),
                pltpu.VMEM((MIX_TILE, GLA_DV), _F32),
            ],
            compiler_params=pltpu.CompilerParams(
                dimension_semantics=("arbitrary", "arbitrary"),
                vmem_limit_bytes=VMEM_LIMIT_BYTES,
            ),
            name="token_mixing",
        )(*mix_inputs)

        ffn_inputs = (
            x.reshape(tokens, D_MODEL),
            p[i].reshape(tokens, PLE_DIM),
            norm_ffn_g[i].reshape(1, D_MODEL),
            ffn_w_up[i].astype(_BF),
            ffn_w_down[i].astype(_BF),
            ple_norm_g[i].reshape(1, D_MODEL),
            ple_w_gate[i].astype(_BF),
            ple_w_proj[i].astype(_BF),
            final_norm_g.reshape(1, D_MODEL),
        )
        tok_spec = pl.BlockSpec((FFN_TILE, D_MODEL), lambda t: (t, 0))
        x = pl.pallas_call(
            _ffn_kernel,
            grid=(tokens // FFN_TILE,),
            in_specs=[tok_spec, pl.BlockSpec((FFN_TILE, PLE_DIM), lambda t: (t, 0))]
            + [_resident(a.shape) for a in ffn_inputs[2:]],
            out_specs=tok_spec,
            out_shape=jax.ShapeDtypeStruct((tokens, D_MODEL), _F32),
            compiler_params=pltpu.CompilerParams(
                dimension_semantics=("arbitrary",),
                vmem_limit_bytes=VMEM_LIMIT_BYTES,
            ),
            name="channel_mixing",
        )(*ffn_inputs).reshape(batch, seq, D_MODEL)
    return x
```

```python
import functools

import jax
import jax.numpy as jnp
from jax import lax
from jax.experimental import pallas as pl
from jax.experimental.pallas import tpu as pltpu

D_MODEL = 1024
PLE_DIM = 256
SG_GROUPS = 4
SG_CHUNK = 128
SG_WIDTH = D_MODEL // 2
SG_GROUP_DIM = SG_WIDTH // SG_GROUPS
GLA_HEADS = 4
GLA_DK = D_MODEL // 2
GLA_DV = D_MODEL
GLA_HEAD_K = GLA_DK // GLA_HEADS
GLA_HEAD_V = GLA_DV // GLA_HEADS
GLA_GATE_RANK = 16
GLA_GATE_TEMP = 16.0
GLA_CHUNK = 64
D_FF = 4 * D_MODEL
EPS = 1e-6

_C_U = 0
_C_Q = _C_U + 2 * SG_WIDTH
_C_LR = _C_Q + 2 * GLA_DK + GLA_DV
_C_OG = _C_LR + GLA_GATE_RANK
_T_BA = GLA_DV
_T_END = _T_BA + 2 * D_MODEL

MIX_TILE = 512
CUMSUM_ROWS = 256
PREP_STEPS = 8
BF16_SUBLANES = 16
FFN_TILE = 512
FFN_SLAB = 1024
VMEM_LIMIT_BYTES = 56 * 1024 * 1024

_BF = jnp.bfloat16
_F32 = jnp.float32

_dot = functools.partial(jnp.dot, preferred_element_type=_F32)


def _rms(x, g):
    ms = jnp.mean(x * x, axis=-1, keepdims=True)
    return x * lax.rsqrt(ms + EPS) * g


def _gelu_tanh(x):
    c = 0.7978845608028654
    return x * (0.5 * (1.0 + jnp.tanh(c * (x + 0.044715 * (x * x * x)))))


def _sigmoid(x):
    return 0.5 * jnp.tanh(0.5 * x) + 0.5


def _log_sigmoid(x):
    return jnp.minimum(x, 0.0) - jnp.log1p(jnp.exp(-jnp.abs(x)))


def _cast_blocks(src_refs, dst_refs):
    for src, dst in zip(src_refs, dst_refs):
        dst[...] = src[...].astype(_BF)


def _prep_mix_weights_kernel(w_in_ref, sg_wout_ref, wgu_ref, gla_wout_ref, wo_ref,
                             head_ref, glr_ref, tail_ref, sg_wout_o, wgu_o, gla_wout_o, wo_o):
    head_ref[...] = w_in_ref[:, :_C_LR].astype(_BF)
    glr_ref[...] = w_in_ref[:, _C_LR:_C_OG].astype(_BF)
    tail_ref[...] = w_in_ref[:, _C_OG:].astype(_BF)
    _cast_blocks((sg_wout_ref, wgu_ref, gla_wout_ref, wo_ref), (sg_wout_o, wgu_o, gla_wout_o, wo_o))


def _mix_kernel(x_ref, g_ref, lmat_ref, w_head_ref, w_tail_ref, w_glr_ref, ln_g_ref, ln_b_ref, ws_ref, bs_ref,
                sg_wout_ref, wgu_ref, bgate_ref, gn_g_ref, gla_wout_ref, wo_ref,
                w_up_f, w_down_f, w_pg_f, w_pp_f,
                out_ref, w_up_o, w_down_o, w_pg_o, w_pp_o, state_ref, o_scr):
    ts = x_ref.shape[0]

    _cast_blocks((w_up_f, w_down_f, w_pg_f, w_pp_f), (w_up_o, w_down_o, w_pg_o, w_pp_o))

    @pl.when(pl.program_id(1) == 0)
    def _():
        state_ref[...] = jnp.zeros_like(state_ref)

    x = x_ref[...]
    hb = _rms(x, g_ref[...]).astype(_BF)

    glr = _dot(hb, w_glr_ref[...])
    uv = _dot(hb, w_head_ref[:, _C_U:_C_Q])
    zg = _dot(glr.astype(_BF), wgu_ref[...]) + bgate_ref[...]
    qkv = _dot(hb, w_head_ref[:, _C_Q:_C_LR])

    log_a = _log_sigmoid(zg) * (1.0 / GLA_GATE_TEMP)
    la_hi = log_a.astype(_BF)
    la_lo = (log_a - la_hi.astype(_F32)).astype(_BF)
    lmat = lmat_ref[...]
    b = jnp.concatenate(
        [_dot(lmat, la_hi[r0:r0 + CUMSUM_ROWS]) + _dot(lmat, la_lo[r0:r0 + CUMSUM_ROWS])
         for r0 in range(0, ts, CUMSUM_ROWS)], axis=0)

    u = _gelu_tanh(uv[:, :SG_WIDTH])
    v = _gelu_tanh(uv[:, SG_WIDTH:])
    mu = jnp.mean(v, axis=-1, keepdims=True)
    vc = v - mu
    var = jnp.mean(vc * vc, axis=-1, keepdims=True)
    vn = (vc * lax.rsqrt(var + EPS) * ln_g_ref[...] + ln_b_ref[...]).astype(_BF)

    og = _dot(hb, w_tail_ref[:, :_T_BA])

    q = qkv[:, :GLA_DK] * (GLA_HEAD_K ** -0.5)
    k = qkv[:, GLA_DK:2 * GLA_DK]
    vb = qkv[:, 2 * GLA_DK:].astype(_BF)
    r64 = lax.broadcasted_iota(jnp.int32, (GLA_CHUNK, GLA_CHUNK), 0)
    c64 = lax.broadcasted_iota(jnp.int32, (GLA_CHUNK, GLA_CHUNK), 1)
    causal = r64 >= c64
    rk = lax.broadcasted_iota(jnp.int32, (GLA_HEAD_K, GLA_HEAD_K), 0)
    ck = lax.broadcasted_iota(jnp.int32, (GLA_HEAD_K, GLA_HEAD_K), 1)
    eye_k = rk == ck
    n_chunks = ts // GLA_CHUNK
    scores_l, kv_l, qdec_l, dcol_l = {}, {}, {}, {}
    for c in range(n_chunks):
        r0 = c * GLA_CHUNK
        bc = b[r0:r0 + GLA_CHUNK]
        b_mid = bc[GLA_CHUNK // 2 - 1:GLA_CHUNK // 2]
        b_last = bc[GLA_CHUNK - 1:GLA_CHUNK]
        qc = q[r0:r0 + GLA_CHUNK]
        kc = k[r0:r0 + GLA_CHUNK]
        q_in = (qc * jnp.exp(bc - b_mid)).astype(_BF)
        k_in = (kc * jnp.exp(b_mid - bc)).astype(_BF)
        k_dec = (kc * jnp.exp(b_last - bc)).astype(_BF)
        qdec_l[c] = (qc * jnp.exp(bc)).astype(_BF)
        dec = jnp.exp(b_last)
        for hd in range(GLA_HEADS):
            ks = slice(hd * GLA_HEAD_K, (hd + 1) * GLA_HEAD_K)
            vs = slice(hd * GLA_HEAD_V, (hd + 1) * GLA_HEAD_V)
            sc = lax.dot_general(q_in[:, ks], k_in[:, ks], (((1,), (1,)), ((), ())),
                                 preferred_element_type=_F32)
            scores_l[c, hd] = jnp.where(causal, sc, 0.0).astype(_BF)
            kv_l[c, hd] = lax.dot_general(k_dec[:, ks], vb[r0:r0 + GLA_CHUNK, vs], (((0,), (0,)), ((), ())),
                                          preferred_element_type=_F32)
            dcol_l[c, hd] = jnp.sum(
                jnp.where(eye_k, jnp.broadcast_to(dec[:, ks], (GLA_HEAD_K, GLA_HEAD_K)), 0.0),
                axis=1, keepdims=True)

    bg = _dot(hb, w_tail_ref[:, _T_BA:_T_END])

    row = lax.broadcasted_iota(jnp.int32, (SG_CHUNK, SG_CHUNK), 0)
    col = lax.broadcasted_iota(jnp.int32, (SG_CHUNK, SG_CHUNK), 1)
    tril = row >= col
    w_sp = [jnp.where(tril, ws_ref[g], 0.0).astype(_BF) for g in range(SG_GROUPS)]
    bias = bs_ref[...]
    mixed_rows = []
    for c in range(ts // SG_CHUNK):
        r0 = c * SG_CHUNK
        blocks = [
            _dot(w_sp[g], vn[r0:r0 + SG_CHUNK, g * SG_GROUP_DIM:(g + 1) * SG_GROUP_DIM])
            for g in range(SG_GROUPS)
        ]
        mixed_rows.append(jnp.concatenate(blocks, axis=1) + bias)
    mixed = jnp.concatenate(mixed_rows, axis=0)
    z = (u * mixed).astype(_BF)
    y_a = _dot(z, sg_wout_ref[...])

    sprev_l = {}
    for hd in range(GLA_HEADS):
        s_run = state_ref[hd]
        for c in range(n_chunks):
            sprev_l[c, hd] = s_run.astype(_BF)
            s_run = dcol_l[c, hd] * s_run + kv_l[c, hd]
        state_ref[hd] = s_run

    for c in range(n_chunks):
        r0 = c * GLA_CHUNK
        for hd in range(GLA_HEADS):
            ks = slice(hd * GLA_HEAD_K, (hd + 1) * GLA_HEAD_K)
            vs = slice(hd * GLA_HEAD_V, (hd + 1) * GLA_HEAD_V)
            o_scr[r0:r0 + GLA_CHUNK, vs] = (_dot(scores_l[c, hd], vb[r0:r0 + GLA_CHUNK, vs])
                                            + _dot(qdec_l[c][:, ks], sprev_l[c, hd]))
    gn_g = gn_g_ref[...]
    o_heads = []
    for hd in range(GLA_HEADS):
        oh = o_scr[:, hd * GLA_HEAD_V:(hd + 1) * GLA_HEAD_V]
        o_heads.append(_rms(oh, gn_g))
    o_n = jnp.concatenate(o_heads, axis=1)
    o_g = (o_n * (og * _sigmoid(og))).astype(_BF)
    y_b = _dot(o_g, gla_wout_ref[...])

    merged = _sigmoid(bg[:, :D_MODEL]) * y_a + _sigmoid(bg[:, D_MODEL:]) * y_b
    out_ref[...] = x + _dot(merged.astype(_BF), wo_ref[...])


def _ffn_kernel(x_ref, p_ref, g_ffn_ref, w_up_ref, w_down_ref, g_ple_ref, w_pg_ref, w_pp_ref,
                g_fin_ref, out_ref):
    x = x_ref[...]
    hb = _rms(x, g_ffn_ref[...]).astype(_BF)
    ple = _dot(p_ref[...].astype(_BF), w_pp_ref[...])
    acc = x
    for f0 in range(0, D_FF, FFN_SLAB):
        up = jnp.maximum(_dot(hb, w_up_ref[:, f0:f0 + FFN_SLAB]), 0.0)
        acc = acc + _dot((up * up).astype(_BF), w_down_ref[f0:f0 + FFN_SLAB, :])
    half = x_ref.shape[0] // 2
    for r0 in (0, half):
        acc_h = acc[r0:r0 + half]
        h3 = _rms(acc_h, g_ple_ref[...]).astype(_BF)
        gate = _sigmoid(_dot(h3, w_pg_ref[...]))
        out_ref[r0:r0 + half, :] = _rms(acc_h + gate * ple[r0:r0 + half], g_fin_ref[...])


def _resident(shape):
    zeros = (0,) * len(shape)
    return pl.BlockSpec(shape, lambda *_: zeros, pipeline_mode=pl.Buffered(1))


def kernel(x, p, norm_mix_g, w_in, sg_ln_g, sg_ln_b, sg_w_s, sg_b_s, sg_w_out, gla_w_gate_up, gla_b_gate, gla_norm_g, gla_w_out, w_o, norm_ffn_g, ffn_w_up, ffn_w_down, ple_norm_g, ple_w_gate, ple_w_proj, final_norm_g):
    batch, seq, d = x.shape
    assert w_in.shape[0] == 1
    assert d == D_MODEL and seq % MIX_TILE == 0 and (batch * seq) % FFN_TILE == 0
    tokens = batch * seq

    for i in range(1):
        def row_slab(arr, steps):
            rows = arr.shape[1] // steps
            return pl.BlockSpec((None, rows) + arr.shape[2:], lambda r: (i, r, 0))

        def row_slab_out(arr, steps, cols=None):
            rows = arr.shape[1] // steps
            return pl.BlockSpec((rows, cols or arr.shape[2]), lambda r: (r, 0))

        whole_gate_up = pl.BlockSpec((None,) + gla_w_gate_up.shape[1:], lambda r: (i, 0, 0))
        tail_cols = w_in.shape[2] - _C_OG
        w_head, w_glr, w_tail, sg_wout_b, wgu_b, gla_wout_b, wo_b = pl.pallas_call(
            _prep_mix_weights_kernel,
            grid=(PREP_STEPS,),
            in_specs=[row_slab(w_in, PREP_STEPS), row_slab(sg_w_out, PREP_STEPS), whole_gate_up,
                      row_slab(gla_w_out, PREP_STEPS), row_slab(w_o, PREP_STEPS)],
            out_specs=[row_slab_out(w_in, PREP_STEPS, _C_LR), row_slab_out(w_in, PREP_STEPS, GLA_GATE_RANK),
                       row_slab_out(w_in, PREP_STEPS, tail_cols), row_slab_out(sg_w_out, PREP_STEPS),
                       pl.BlockSpec(gla_w_gate_up.shape[1:], lambda r: (0, 0)),
                       row_slab_out(gla_w_out, PREP_STEPS), row_slab_out(w_o, PREP_STEPS)],
            out_shape=[jax.ShapeDtypeStruct((D_MODEL, _C_LR), _BF),
                       jax.ShapeDtypeStruct((D_MODEL, GLA_GATE_RANK), _BF),
                       jax.ShapeDtypeStruct((D_MODEL, tail_cols), _BF),
                       jax.ShapeDtypeStruct(sg_w_out.shape[1:], _BF),
                       jax.ShapeDtypeStruct(gla_w_gate_up.shape[1:], _BF),
                       jax.ShapeDtypeStruct(gla_w_out.shape[1:], _BF),
                       jax.ShapeDtypeStruct(w_o.shape[1:], _BF)],
            compiler_params=pltpu.CompilerParams(dimension_semantics=("arbitrary",)),
            name="mix_weight_prep",
        )(w_in, sg_w_out, gla_w_gate_up, gla_w_out, w_o)

        bias_sp = jnp.repeat(sg_b_s[i].T, SG_GROUP_DIM, axis=1)
        pos = jnp.arange(CUMSUM_ROWS)
        lmat = ((pos[:, None] // GLA_CHUNK == pos[None, :] // GLA_CHUNK) & (pos[None, :] <= pos[:, None])).astype(_BF)

        resident_inputs = (
            norm_mix_g[i].reshape(1, D_MODEL),
            lmat,
            w_head,
            w_tail,
            w_glr,
            sg_ln_g[i].reshape(1, SG_WIDTH),
            sg_ln_b[i].reshape(1, SG_WIDTH),
            sg_w_s[i],
            bias_sp,
            sg_wout_b,
            wgu_b,
            gla_b_gate[i].reshape(1, GLA_DK),
            gla_norm_g[i].reshape(1, GLA_HEAD_V),
            gla_wout_b,
            wo_b,
        )
        n_seq = seq // MIX_TILE
        mix_steps = batch * n_seq
        side_weights = (ffn_w_up, ffn_w_down, ple_w_gate, ple_w_proj)

        def slab_rows(arr):
            return max(arr.shape[1] // mix_steps, BF16_SUBLANES)

        def side_in(arr):
            rows = slab_rows(arr)
            last = arr.shape[1] // rows - 1
            return pl.BlockSpec((None, rows, arr.shape[2]),
                                lambda bi, si: (i, jnp.minimum(bi * n_seq + si, last), 0))

        def side_out(arr):
            rows = slab_rows(arr)
            last = arr.shape[1] // rows - 1
            return pl.BlockSpec((rows, arr.shape[2]), lambda bi, si: (jnp.minimum(bi * n_seq + si, last), 0))

        x_spec = pl.BlockSpec((None, MIX_TILE, D_MODEL), lambda bi, si: (bi, si, 0))
        x, w_up_b, w_down_b, w_pg_b, w_pp_b = pl.pallas_call(
            _mix_kernel,
            grid=(batch, n_seq),
            in_specs=[x_spec] + [_resident(a.shape) for a in resident_inputs] + [side_in(a) for a in side_weights],
            out_specs=[x_spec] + [side_out(a) for a in side_weights],
            out_shape=[jax.ShapeDtypeStruct((batch, seq, D_MODEL), _F32)]
            + [jax.ShapeDtypeStruct(a.shape[1:], _BF) for a in side_weights],
            scratch_shapes=[
                pltpu.VMEM((GLA_HEADS, GLA_HEAD_K, GLA_HEAD_V), _F32),
                pltpu.VMEM((MIX_TILE, GLA_DV), _F32),
            ],
            compiler_params=pltpu.CompilerParams(
                dimension_semantics=("arbitrary", "arbitrary"),
                vmem_limit_bytes=VMEM_LIMIT_BYTES,
            ),
            name="token_mixing",
        )(x, *resident_inputs, *side_weights)

        ffn_inputs = (
            x.reshape(tokens, D_MODEL),
            p[i].reshape(tokens, PLE_DIM),
            norm_ffn_g[i].reshape(1, D_MODEL),
            w_up_b,
            w_down_b,
            ple_norm_g[i].reshape(1, D_MODEL),
            w_pg_b,
            w_pp_b,
            final_norm_g.reshape(1, D_MODEL),
        )
        tok_spec = pl.BlockSpec((FFN_TILE, D_MODEL), lambda t: (t, 0))
        x = pl.pallas_call(
            _ffn_kernel,
            grid=(tokens // FFN_TILE,),
            in_specs=[tok_spec, pl.BlockSpec((FFN_TILE, PLE_DIM), lambda t: (t, 0))]
            + [_resident(a.shape) for a in ffn_inputs[2:]],
            out_specs=tok_spec,
            out_shape=jax.ShapeDtypeStruct((tokens, D_MODEL), _F32),
            compiler_params=pltpu.CompilerParams(
                dimension_semantics=("arbitrary",),
                vmem_limit_bytes=VMEM_LIMIT_BYTES,
            ),
            name="channel_mixing",
        )(*ffn_inputs).reshape(batch, seq, D_MODEL)
    return x
```

```python
import functools

import jax
import jax.numpy as jnp
from jax import lax
from jax.experimental import pallas as pl
from jax.experimental.pallas import tpu as pltpu

D_MODEL = 1024
PLE_DIM = 256
SG_GROUPS = 4
SG_CHUNK = 128
SG_WIDTH = D_MODEL // 2
SG_GROUP_DIM = SG_WIDTH // SG_GROUPS
GLA_HEADS = 4
GLA_DK = D_MODEL // 2
GLA_DV = D_MODEL
GLA_HEAD_K = GLA_DK // GLA_HEADS
GLA_HEAD_V = GLA_DV // GLA_HEADS
GLA_GATE_RANK = 16
GLA_GATE_TEMP = 16.0
GLA_CHUNK = 64
D_FF = 4 * D_MODEL
EPS = 1e-6

_C_U = 0
_C_Q = _C_U + 2 * SG_WIDTH
_C_LR = _C_Q + 2 * GLA_DK + GLA_DV
_C_OG = _C_LR + GLA_GATE_RANK
_C_BA = _C_OG + GLA_DV
_C_END = _C_BA + 2 * D_MODEL

MIX_TILE = 512
CUMSUM_ROWS = 256
PREP_STEPS = 7
BF16_SUBLANES = 16
FFN_TILE = 512
FFN_SLAB = 1024
VMEM_LIMIT_BYTES = 56 * 1024 * 1024

_BF = jnp.bfloat16
_F32 = jnp.float32

_dot = functools.partial(jnp.dot, preferred_element_type=_F32)


def _rms(x, g):
    ms = jnp.mean(x * x, axis=-1, keepdims=True)
    return x * lax.rsqrt(ms + EPS) * g


def _gelu_tanh(x):
    c = 0.7978845608028654
    return x * (0.5 * (1.0 + jnp.tanh(c * (x + 0.044715 * (x * x * x)))))


def _sigmoid(x):
    return 0.5 * jnp.tanh(0.5 * x) + 0.5


def _log_sigmoid(x):
    return jnp.minimum(x, 0.0) - jnp.log1p(jnp.exp(-jnp.abs(x)))


def _cast_blocks(src_refs, dst_refs):
    for src, dst in zip(src_refs, dst_refs):
        dst[...] = src[...].astype(_BF)


def _prep_mix_weights_kernel(*refs):
    n = len(refs) // 2
    _cast_blocks(refs[:n], refs[n:])


def _mix_kernel(x_ref, g_ref, lmat_ref, w_in_t_ref, ln_g_ref, ln_b_ref, ws_ref, bs_ref,
                sg_wout_ref, wgu_ref, bgate_ref, gn_g_ref, gla_wout_ref, wo_ref,
                w_up_f, w_down_f, w_pg_f, w_pp_f,
                out_ref, w_up_o, w_down_o, w_pg_o, w_pp_o, state_ref, o_scr):
    ts = x_ref.shape[0]

    def proj(lo, hi):
        return lax.dot_general(hb, w_in_t_ref[lo:hi, :], (((1,), (1,)), ((), ())), preferred_element_type=_F32)

    _cast_blocks((w_up_f, w_down_f, w_pg_f, w_pp_f), (w_up_o, w_down_o, w_pg_o, w_pp_o))

    @pl.when(pl.program_id(1) == 0)
    def _():
        state_ref[...] = jnp.zeros_like(state_ref)

    x = x_ref[...]
    hb = _rms(x, g_ref[...]).astype(_BF)

    glr = proj(_C_LR, _C_OG)
    uv = proj(_C_U, _C_Q)
    zg = _dot(glr.astype(_BF), wgu_ref[...]) + bgate_ref[...]
    qkv = proj(_C_Q, _C_LR)

    log_a = _log_sigmoid(zg) * (1.0 / GLA_GATE_TEMP)
    la_hi = log_a.astype(_BF)
    la_lo = (log_a - la_hi.astype(_F32)).astype(_BF)
    lmat = lmat_ref[...]
    b = jnp.concatenate(
        [_dot(lmat, la_hi[r0:r0 + CUMSUM_ROWS]) + _dot(lmat, la_lo[r0:r0 + CUMSUM_ROWS])
         for r0 in range(0, ts, CUMSUM_ROWS)], axis=0)

    u = _gelu_tanh(uv[:, :SG_WIDTH])
    v = _gelu_tanh(uv[:, SG_WIDTH:])
    mu = jnp.mean(v, axis=-1, keepdims=True)
    vc = v - mu
    var = jnp.mean(vc * vc, axis=-1, keepdims=True)
    vn = (vc * lax.rsqrt(var + EPS) * ln_g_ref[...] + ln_b_ref[...]).astype(_BF)

    og = proj(_C_OG, _C_BA)

    q = qkv[:, :GLA_DK] * (GLA_HEAD_K ** -0.5)
    k = qkv[:, GLA_DK:2 * GLA_DK]
    vb = qkv[:, 2 * GLA_DK:].astype(_BF)
    r64 = lax.broadcasted_iota(jnp.int32, (GLA_CHUNK, GLA_CHUNK), 0)
    c64 = lax.broadcasted_iota(jnp.int32, (GLA_CHUNK, GLA_CHUNK), 1)
    causal = r64 >= c64
    rk = lax.broadcasted_iota(jnp.int32, (GLA_HEAD_K, GLA_HEAD_K), 0)
    ck = lax.broadcasted_iota(jnp.int32, (GLA_HEAD_K, GLA_HEAD_K), 1)
    eye_k = rk == ck
    n_chunks = ts // GLA_CHUNK
    scores_l, kv_l, qdec_l, dcol_l = {}, {}, {}, {}
    for c in range(n_chunks):
        r0 = c * GLA_CHUNK
        bc = b[r0:r0 + GLA_CHUNK]
        b_mid = bc[GLA_CHUNK // 2 - 1:GLA_CHUNK // 2]
        b_last = bc[GLA_CHUNK - 1:GLA_CHUNK]
        qc = q[r0:r0 + GLA_CHUNK]
        kc = k[r0:r0 + GLA_CHUNK]
        q_in = (qc * jnp.exp(bc - b_mid)).astype(_BF)
        k_in = (kc * jnp.exp(b_mid - bc)).astype(_BF)
        k_dec = (kc * jnp.exp(b_last - bc)).astype(_BF)
        qdec_l[c] = (qc * jnp.exp(bc)).astype(_BF)
        dec = jnp.exp(b_last)
        for hd in range(GLA_HEADS):
            ks = slice(hd * GLA_HEAD_K, (hd + 1) * GLA_HEAD_K)
            vs = slice(hd * GLA_HEAD_V, (hd + 1) * GLA_HEAD_V)
            sc = lax.dot_general(q_in[:, ks], k_in[:, ks], (((1,), (1,)), ((), ())),
                                 preferred_element_type=_F32)
            scores_l[c, hd] = jnp.where(causal, sc, 0.0).astype(_BF)
            kv_l[c, hd] = lax.dot_general(k_dec[:, ks], vb[r0:r0 + GLA_CHUNK, vs], (((0,), (0,)), ((), ())),
                                          preferred_element_type=_F32)
            dcol_l[c, hd] = jnp.sum(
                jnp.where(eye_k, jnp.broadcast_to(dec[:, ks], (GLA_HEAD_K, GLA_HEAD_K)), 0.0),
                axis=1, keepdims=True)

    bg = proj(_C_BA, _C_END)

    row = lax.broadcasted_iota(jnp.int32, (SG_CHUNK, SG_CHUNK), 0)
    col = lax.broadcasted_iota(jnp.int32, (SG_CHUNK, SG_CHUNK), 1)
    tril = row >= col
    w_sp = [jnp.where(tril, ws_ref[g], 0.0).astype(_BF) for g in range(SG_GROUPS)]
    bias = bs_ref[...]
    mixed_rows = []
    for c in range(ts // SG_CHUNK):
        r0 = c * SG_CHUNK
        blocks = [
            _dot(w_sp[g], vn[r0:r0 + SG_CHUNK, g * SG_GROUP_DIM:(g + 1) * SG_GROUP_DIM])
            for g in range(SG_GROUPS)
        ]
        mixed_rows.append(jnp.concatenate(blocks, axis=1) + bias)
    mixed = jnp.concatenate(mixed_rows, axis=0)
    z = (u * mixed).astype(_BF)
    y_a = _dot(z, sg_wout_ref[...])

    sprev_l = {}
    for hd in range(GLA_HEADS):
        s_run = state_ref[hd]
        for c in range(n_chunks):
            sprev_l[c, hd] = s_run.astype(_BF)
            s_run = dcol_l[c, hd] * s_run + kv_l[c, hd]
        state_ref[hd] = s_run

    for c in range(n_chunks):
        r0 = c * GLA_CHUNK
        for hd in range(GLA_HEADS):
            ks = slice(hd * GLA_HEAD_K, (hd + 1) * GLA_HEAD_K)
            vs = slice(hd * GLA_HEAD_V, (hd + 1) * GLA_HEAD_V)
            o_scr[r0:r0 + GLA_CHUNK, vs] = (_dot(scores_l[c, hd], vb[r0:r0 + GLA_CHUNK, vs])
                                            + _dot(qdec_l[c][:, ks], sprev_l[c, hd]))
    gn_g = gn_g_ref[...]
    o_heads = []
    for hd in range(GLA_HEADS):
        oh = o_scr[:, hd * GLA_HEAD_V:(hd + 1) * GLA_HEAD_V]
        o_heads.append(_rms(oh, gn_g))
    o_n = jnp.concatenate(o_heads, axis=1)
    o_g = (o_n * (og * _sigmoid(og))).astype(_BF)
    y_b = _dot(o_g, gla_wout_ref[...])

    merged = _sigmoid(bg[:, :D_MODEL]) * y_a + _sigmoid(bg[:, D_MODEL:]) * y_b
    out_ref[...] = x + _dot(merged.astype(_BF), wo_ref[...])


def _ffn_kernel(x_ref, p_ref, g_ffn_ref, w_up_ref, w_down_ref, g_ple_ref, w_pg_ref, w_pp_ref,
                g_fin_ref, out_ref):
    x = x_ref[...]
    hb = _rms(x, g_ffn_ref[...]).astype(_BF)
    ple = _dot(p_ref[...].astype(_BF), w_pp_ref[...])
    acc = x
    for f0 in range(0, D_FF, FFN_SLAB):
        up = jnp.maximum(_dot(hb, w_up_ref[:, f0:f0 + FFN_SLAB]), 0.0)
        acc = acc + _dot((up * up).astype(_BF), w_down_ref[f0:f0 + FFN_SLAB, :])
    half = x_ref.shape[0] // 2
    for r0 in (0, half):
        acc_h = acc[r0:r0 + half]
        h3 = _rms(acc_h, g_ple_ref[...]).astype(_BF)
        gate = _sigmoid(_dot(h3, w_pg_ref[...]))
        out_ref[r0:r0 + half, :] = _rms(acc_h + gate * ple[r0:r0 + half], g_fin_ref[...])


def _resident(shape):
    zeros = (0,) * len(shape)
    return pl.BlockSpec(shape, lambda *_: zeros, pipeline_mode=pl.Buffered(1))


def _slab_specs(arr, steps, step_of):
    rows = arr.shape[1]
    n = max(k for k in range(1, steps + 1) if rows % k == 0 and (rows // k) % BF16_SUBLANES == 0)
    in_spec = pl.BlockSpec((None, rows // n, arr.shape[2]), lambda *g: (0, jnp.minimum(step_of(*g), n - 1), 0))
    out_spec = pl.BlockSpec((rows // n, arr.shape[2]), lambda *g: (jnp.minimum(step_of(*g), n - 1), 0))
    return in_spec, out_spec


def _bf16_like(arr):
    return jax.ShapeDtypeStruct(arr.shape[1:], _BF)


def kernel(x, p, norm_mix_g, w_in, sg_ln_g, sg_ln_b, sg_w_s, sg_b_s, sg_w_out, gla_w_gate_up, gla_b_gate, gla_norm_g, gla_w_out, w_o, norm_ffn_g, ffn_w_up, ffn_w_down, ple_norm_g, ple_w_gate, ple_w_proj, final_norm_g):
    batch, seq, d = x.shape
    assert w_in.shape[0] == 1
    assert d == D_MODEL and seq % MIX_TILE == 0 and (batch * seq) % FFN_TILE == 0
    tokens = batch * seq
    i = 0

    mix_weights = (jnp.swapaxes(w_in, 1, 2), sg_w_out, gla_w_gate_up, gla_w_out, w_o)
    prep_specs = [_slab_specs(a, PREP_STEPS, lambda r: r) for a in mix_weights]
    w_in_t, sg_wout_b, wgu_b, gla_wout_b, wo_b = pl.pallas_call(
        _prep_mix_weights_kernel,
        grid=(PREP_STEPS,),
        in_specs=[sp[0] for sp in prep_specs],
        out_specs=[sp[1] for sp in prep_specs],
        out_shape=[_bf16_like(a) for a in mix_weights],
        compiler_params=pltpu.CompilerParams(dimension_semantics=("arbitrary",)),
        name="mix_weight_prep",
    )(*mix_weights)

    bias_sp = jnp.repeat(sg_b_s[i].T, SG_GROUP_DIM, axis=1)
    pos = jnp.arange(CUMSUM_ROWS)
    lmat = ((pos[:, None] // GLA_CHUNK == pos[None, :] // GLA_CHUNK) & (pos[None, :] <= pos[:, None])).astype(_BF)

    resident_inputs = (
        norm_mix_g[i].reshape(1, D_MODEL),
        lmat,
        w_in_t,
        sg_ln_g[i].reshape(1, SG_WIDTH),
        sg_ln_b[i].reshape(1, SG_WIDTH),
        sg_w_s[i],
        bias_sp,
        sg_wout_b,
        wgu_b,
        gla_b_gate[i].reshape(1, GLA_DK),
        gla_norm_g[i].reshape(1, GLA_HEAD_V),
        gla_wout_b,
        wo_b,
    )
    n_seq = seq // MIX_TILE
    side_weights = (ffn_w_up, ffn_w_down, ple_w_gate, ple_w_proj)
    side_specs = [_slab_specs(a, batch * n_seq, lambda bi, si: bi * n_seq + si) for a in side_weights]
    x_spec = pl.BlockSpec((None, MIX_TILE, D_MODEL), lambda bi, si: (bi, si, 0))
    x, w_up_b, w_down_b, w_pg_b, w_pp_b = pl.pallas_call(
        _mix_kernel,
        grid=(batch, n_seq),
        in_specs=[x_spec] + [_resident(a.shape) for a in resident_inputs] + [sp[0] for sp in side_specs],
        out_specs=[x_spec] + [sp[1] for sp in side_specs],
        out_shape=[jax.ShapeDtypeStruct((batch, seq, D_MODEL), _F32)] + [_bf16_like(a) for a in side_weights],
        scratch_shapes=[
            pltpu.VMEM((GLA_HEADS, GLA_HEAD_K, GLA_HEAD_V), _F32),
            pltpu.VMEM((MIX_TILE, GLA_DV), _F32),
        ],
        compiler_params=pltpu.CompilerParams(
            dimension_semantics=("arbitrary", "arbitrary"),
            vmem_limit_bytes=VMEM_LIMIT_BYTES,
        ),
        name="token_mixing",
    )(x, *resident_inputs, *side_weights)

    ffn_inputs = (
        x.reshape(tokens, D_MODEL),
        p[i].reshape(tokens, PLE_DIM),
        norm_ffn_g[i].reshape(1, D_MODEL),
        w_up_b,
        w_down_b,
        ple_norm_g[i].reshape(1, D_MODEL),
        w_pg_b,
        w_pp_b,
        final_norm_g.reshape(1, D_MODEL),
    )
    tok_spec = pl.BlockSpec((FFN_TILE, D_MODEL), lambda t: (t, 0))
    out = pl.pallas_call(
        _ffn_kernel,
        grid=(tokens // FFN_TILE,),
        in_specs=[tok_spec, pl.BlockSpec((FFN_TILE, PLE_DIM), lambda t: (t, 0))]
        + [_resident(a.shape) for a in ffn_inputs[2:]],
        out_specs=tok_spec,
        out_shape=jax.ShapeDtypeStruct((tokens, D_MODEL), _F32),
        compiler_params=pltpu.CompilerParams(
            dimension_semantics=("arbitrary",),
            vmem_limit_bytes=VMEM_LIMIT_BYTES,
        ),
        name="channel_mixing",
    )(*ffn_inputs)
    return out.reshape(batch, seq, D_MODEL)
```

```python
import functools

import jax
import jax.numpy as jnp
from jax import lax
from jax.experimental import pallas as pl
from jax.experimental.pallas import tpu as pltpu

D_MODEL = 1024
PLE_DIM = 256
SG_GROUPS = 4
SG_CHUNK = 128
SG_WIDTH = D_MODEL // 2
SG_GROUP_DIM = SG_WIDTH // SG_GROUPS
GLA_HEADS = 4
GLA_DK = D_MODEL // 2
GLA_DV = D_MODEL
GLA_HEAD_K = GLA_DK // GLA_HEADS
GLA_HEAD_V = GLA_DV // GLA_HEADS
GLA_GATE_RANK = 16
GLA_GATE_TEMP = 16.0
GLA_CHUNK = 64
D_FF = 4 * D_MODEL
EPS = 1e-6

_C_U = 0
_C_Q = _C_U + 2 * SG_WIDTH
_C_LR = _C_Q + 2 * GLA_DK + GLA_DV
_C_OG = _C_LR + GLA_GATE_RANK
_C_BA = _C_OG + GLA_DV
_C_END = _C_BA + 2 * D_MODEL

MIX_TILE = 512
CUMSUM_ROWS = 256
PREP_STEPS = 7
BF16_SUBLANES = 16
FFN_TILE = 1024
FFN_SLAB = 1024
VMEM_LIMIT_BYTES = 56 * 1024 * 1024

_BF = jnp.bfloat16
_F32 = jnp.float32

_dot = functools.partial(jnp.dot, preferred_element_type=_F32)


def _rms(x, g):
    ms = jnp.mean(x * x, axis=-1, keepdims=True)
    return x * lax.rsqrt(ms + EPS) * g


def _gelu_tanh(x):
    c = 0.7978845608028654
    return x * (0.5 * (1.0 + jnp.tanh(c * (x + 0.044715 * (x * x * x)))))


def _sigmoid(x):
    return 0.5 * jnp.tanh(0.5 * x) + 0.5


def _log_sigmoid(x):
    return jnp.minimum(x, 0.0) - jnp.log1p(jnp.exp(-jnp.abs(x)))


def _cast_blocks(src_refs, dst_refs):
    for src, dst in zip(src_refs, dst_refs):
        dst[...] = src[...].astype(_BF)


def _prep_mix_weights_kernel(*refs):
    n = len(refs) // 2
    _cast_blocks(refs[:n], refs[n:])


def _mix_kernel(x_ref, g_ref, lmat_ref, w_in_t_ref, ln_g_ref, ln_b_ref, ws_ref, bs_ref,
                sg_wout_ref, wgu_ref, bgate_ref, gn_g_ref, gla_wout_ref, wo_ref,
                w_up_f, w_down_f, w_pg_f, w_pp_f,
                out_ref, w_up_o, w_down_o, w_pg_o, w_pp_o, state_ref, o_scr):
    ts = x_ref.shape[0]

    def proj(lo, hi):
        return lax.dot_general(hb, w_in_t_ref[lo:hi, :], (((1,), (1,)), ((), ())), preferred_element_type=_F32)

    _cast_blocks((w_up_f, w_down_f, w_pg_f, w_pp_f), (w_up_o, w_down_o, w_pg_o, w_pp_o))

    @pl.when(pl.program_id(1) == 0)
    def _():
        state_ref[...] = jnp.zeros_like(state_ref)

    x = x_ref[...]
    hb = _rms(x, g_ref[...]).astype(_BF)

    glr = proj(_C_LR, _C_OG)
    uv = proj(_C_U, _C_Q)
    zg = _dot(glr.astype(_BF), wgu_ref[...]) + bgate_ref[...]
    qkv = proj(_C_Q, _C_LR)

    log_a = _log_sigmoid(zg) * (1.0 / GLA_GATE_TEMP)
    la_hi = log_a.astype(_BF)
    la_lo = (log_a - la_hi.astype(_F32)).astype(_BF)
    lmat = lmat_ref[...]
    b = jnp.concatenate(
        [_dot(lmat, la_hi[r0:r0 + CUMSUM_ROWS]) + _dot(lmat, la_lo[r0:r0 + CUMSUM_ROWS])
         for r0 in range(0, ts, CUMSUM_ROWS)], axis=0)

    u = _gelu_tanh(uv[:, :SG_WIDTH])
    v = _gelu_tanh(uv[:, SG_WIDTH:])
    mu = jnp.mean(v, axis=-1, keepdims=True)
    vc = v - mu
    var = jnp.mean(vc * vc, axis=-1, keepdims=True)
    vn = (vc * lax.rsqrt(var + EPS) * ln_g_ref[...] + ln_b_ref[...]).astype(_BF)

    og = proj(_C_OG, _C_BA)

    q = qkv[:, :GLA_DK] * (GLA_HEAD_K ** -0.5)
    k = qkv[:, GLA_DK:2 * GLA_DK]
    vb = qkv[:, 2 * GLA_DK:].astype(_BF)
    r64 = lax.broadcasted_iota(jnp.int32, (GLA_CHUNK, GLA_CHUNK), 0)
    c64 = lax.broadcasted_iota(jnp.int32, (GLA_CHUNK, GLA_CHUNK), 1)
    causal = r64 >= c64
    rk = lax.broadcasted_iota(jnp.int32, (GLA_HEAD_K, GLA_HEAD_K), 0)
    ck = lax.broadcasted_iota(jnp.int32, (GLA_HEAD_K, GLA_HEAD_K), 1)
    eye_k = rk == ck
    n_chunks = ts // GLA_CHUNK
    scores_l, kv_l, qdec_l, dcol_l = {}, {}, {}, {}
    for c in range(n_chunks):
        r0 = c * GLA_CHUNK
        bc = b[r0:r0 + GLA_CHUNK]
        b_mid = bc[GLA_CHUNK // 2 - 1:GLA_CHUNK // 2]
        b_last = bc[GLA_CHUNK - 1:GLA_CHUNK]
        qc = q[r0:r0 + GLA_CHUNK]
        kc = k[r0:r0 + GLA_CHUNK]
        q_in = (qc * jnp.exp(bc - b_mid)).astype(_BF)
        k_in = (kc * jnp.exp(b_mid - bc)).astype(_BF)
        k_dec = (kc * jnp.exp(b_last - bc)).astype(_BF)
        qdec_l[c] = (qc * jnp.exp(bc)).astype(_BF)
        dec = jnp.exp(b_last)
        for hd in range(GLA_HEADS):
            ks = slice(hd * GLA_HEAD_K, (hd + 1) * GLA_HEAD_K)
            vs = slice(hd * GLA_HEAD_V, (hd + 1) * GLA_HEAD_V)
            sc = lax.dot_general(q_in[:, ks], k_in[:, ks], (((1,), (1,)), ((), ())),
                                 preferred_element_type=_F32)
            scores_l[c, hd] = jnp.where(causal, sc, 0.0).astype(_BF)
            kv_l[c, hd] = lax.dot_general(k_dec[:, ks], vb[r0:r0 + GLA_CHUNK, vs], (((0,), (0,)), ((), ())),
                                          preferred_element_type=_F32)
            dcol_l[c, hd] = jnp.sum(
                jnp.where(eye_k, jnp.broadcast_to(dec[:, ks], (GLA_HEAD_K, GLA_HEAD_K)), 0.0),
                axis=1, keepdims=True)

    bg = proj(_C_BA, _C_END)

    row = lax.broadcasted_iota(jnp.int32, (SG_CHUNK, SG_CHUNK), 0)
    col = lax.broadcasted_iota(jnp.int32, (SG_CHUNK, SG_CHUNK), 1)
    tril = row >= col
    w_sp = [jnp.where(tril, ws_ref[g], 0.0).astype(_BF) for g in range(SG_GROUPS)]
    bias = bs_ref[...]
    mixed_rows = []
    for c in range(ts // SG_CHUNK):
        r0 = c * SG_CHUNK
        blocks = [
            _dot(w_sp[g], vn[r0:r0 + SG_CHUNK, g * SG_GROUP_DIM:(g + 1) * SG_GROUP_DIM])
            for g in range(SG_GROUPS)
        ]
        mixed_rows.append(jnp.concatenate(blocks, axis=1) + bias)
    mixed = jnp.concatenate(mixed_rows, axis=0)
    z = (u * mixed).astype(_BF)
    y_a = _dot(z, sg_wout_ref[...])

    sprev_l = {}
    for hd in range(GLA_HEADS):
        s_run = state_ref[hd]
        for c in range(n_chunks):
            sprev_l[c, hd] = s_run.astype(_BF)
            s_run = dcol_l[c, hd] * s_run + kv_l[c, hd]
        state_ref[hd] = s_run

    for c in range(n_chunks):
        r0 = c * GLA_CHUNK
        for hd in range(GLA_HEADS):
            ks = slice(hd * GLA_HEAD_K, (hd + 1) * GLA_HEAD_K)
            vs = slice(hd * GLA_HEAD_V, (hd + 1) * GLA_HEAD_V)
            o_scr[r0:r0 + GLA_CHUNK, vs] = (_dot(scores_l[c, hd], vb[r0:r0 + GLA_CHUNK, vs])
                                            + _dot(qdec_l[c][:, ks], sprev_l[c, hd]))
    gn_g = gn_g_ref[...]
    o_heads = []
    for hd in range(GLA_HEADS):
        oh = o_scr[:, hd * GLA_HEAD_V:(hd + 1) * GLA_HEAD_V]
        o_heads.append(_rms(oh, gn_g))
    o_n = jnp.concatenate(o_heads, axis=1)
    o_g = (o_n * (og * _sigmoid(og))).astype(_BF)
    y_b = _dot(o_g, gla_wout_ref[...])

    merged = _sigmoid(bg[:, :D_MODEL]) * y_a + _sigmoid(bg[:, D_MODEL:]) * y_b
    out_ref[...] = x + _dot(merged.astype(_BF), wo_ref[...])


def _ffn_kernel(x_ref, p_ref, g_ffn_ref, w_up_ref, w_down_ref, g_ple_ref, w_pg_ref, w_pp_ref,
                g_fin_ref, out_ref):
    x = x_ref[...]
    hb = _rms(x, g_ffn_ref[...]).astype(_BF)
    ple = _dot(p_ref[...].astype(_BF), w_pp_ref[...])
    acc = x
    for f0 in range(0, D_FF, FFN_SLAB):
        up = jnp.maximum(_dot(hb, w_up_ref[:, f0:f0 + FFN_SLAB]), 0.0)
        acc = acc + _dot((up * up).astype(_BF), w_down_ref[f0:f0 + FFN_SLAB, :])
    half = x_ref.shape[0] // 2
    for r0 in (0, half):
        acc_h = acc[r0:r0 + half]
        h3 = _rms(acc_h, g_ple_ref[...]).astype(_BF)
        gate = _sigmoid(_dot(h3, w_pg_ref[...]))
        out_ref[r0:r0 + half, :] = _rms(acc_h + gate * ple[r0:r0 + half], g_fin_ref[...])


def _resident(shape):
    zeros = (0,) * len(shape)
    return pl.BlockSpec(shape, lambda *_: zeros, pipeline_mode=pl.Buffered(1))


def _slab_specs(arr, steps, step_of):
    rows = arr.shape[1]
    n = max(k for k in range(1, steps + 1) if rows % k == 0 and (rows // k) % BF16_SUBLANES == 0)
    in_spec = pl.BlockSpec((None, rows // n, arr.shape[2]), lambda *g: (0, jnp.minimum(step_of(*g), n - 1), 0))
    out_spec = pl.BlockSpec((rows // n, arr.shape[2]), lambda *g: (jnp.minimum(step_of(*g), n - 1), 0))
    return in_spec, out_spec


def _bf16_like(arr):
    return jax.ShapeDtypeStruct(arr.shape[1:], _BF)


def kernel(x, p, norm_mix_g, w_in, sg_ln_g, sg_ln_b, sg_w_s, sg_b_s, sg_w_out, gla_w_gate_up, gla_b_gate, gla_norm_g, gla_w_out, w_o, norm_ffn_g, ffn_w_up, ffn_w_down, ple_norm_g, ple_w_gate, ple_w_proj, final_norm_g):
    batch, seq, d = x.shape
    assert w_in.shape[0] == 1
    assert d == D_MODEL and seq % MIX_TILE == 0 and (batch * seq) % FFN_TILE == 0
    tokens = batch * seq
    i = 0

    mix_weights = (jnp.swapaxes(w_in, 1, 2), sg_w_out, gla_w_gate_up, gla_w_out, w_o)
    prep_specs = [_slab_specs(a, PREP_STEPS, lambda r: r) for a in mix_weights]
    w_in_t, sg_wout_b, wgu_b, gla_wout_b, wo_b = pl.pallas_call(
        _prep_mix_weights_kernel,
        grid=(PREP_STEPS,),
        in_specs=[sp[0] for sp in prep_specs],
        out_specs=[sp[1] for sp in prep_specs],
        out_shape=[_bf16_like(a) for a in mix_weights],
        compiler_params=pltpu.CompilerParams(dimension_semantics=("arbitrary",)),
        name="mix_weight_prep",
    )(*mix_weights)

    bias_sp = jnp.repeat(sg_b_s[i].T, SG_GROUP_DIM, axis=1)
    pos = jnp.arange(CUMSUM_ROWS)
    lmat = ((pos[:, None] // GLA_CHUNK == pos[None, :] // GLA_CHUNK) & (pos[None, :] <= pos[:, None])).astype(_BF)

    resident_inputs = (
        norm_mix_g[i].reshape(1, D_MODEL),
        lmat,
        w_in_t,
        sg_ln_g[i].reshape(1, SG_WIDTH),
        sg_ln_b[i].reshape(1, SG_WIDTH),
        sg_w_s[i],
        bias_sp,
        sg_wout_b,
        wgu_b,
        gla_b_gate[i].reshape(1, GLA_DK),
        gla_norm_g[i].reshape(1, GLA_HEAD_V),
        gla_wout_b,
        wo_b,
    )
    n_seq = seq // MIX_TILE
    side_weights = (ffn_w_up, ffn_w_down, ple_w_gate, ple_w_proj)
    side_specs = [_slab_specs(a, batch * n_seq, lambda bi, si: bi * n_seq + si) for a in side_weights]
    x_spec = pl.BlockSpec((None, MIX_TILE, D_MODEL), lambda bi, si: (bi, si, 0))
    x, w_up_b, w_down_b, w_pg_b, w_pp_b = pl.pallas_call(
        _mix_kernel,
        grid=(batch, n_seq),
        in_specs=[x_spec] + [_resident(a.shape) for a in resident_inputs] + [sp[0] for sp in side_specs],
        out_specs=[x_spec] + [sp[1] for sp in side_specs],
        out_shape=[jax.ShapeDtypeStruct((batch, seq, D_MODEL), _F32)] + [_bf16_like(a) for a in side_weights],
        scratch_shapes=[
            pltpu.VMEM((GLA_HEADS, GLA_HEAD_K, GLA_HEAD_V), _F32),
            pltpu.VMEM((MIX_TILE, GLA_DV), _F32),
        ],
        compiler_params=pltpu.CompilerParams(
            dimension_semantics=("arbitrary", "arbitrary"),
            vmem_limit_bytes=VMEM_LIMIT_BYTES,
        ),
        name="token_mixing",
    )(x, *resident_inputs, *side_weights)

    ffn_inputs = (
        x.reshape(tokens, D_MODEL),
        p[i].reshape(tokens, PLE_DIM),
        norm_ffn_g[i].reshape(1, D_MODEL),
        w_up_b,
        w_down_b,
        ple_norm_g[i].reshape(1, D_MODEL),
        w_pg_b,
        w_pp_b,
        final_norm_g.reshape(1, D_MODEL),
    )
    tok_spec = pl.BlockSpec((FFN_TILE, D_MODEL), lambda t: (t, 0))
    out = pl.pallas_call(
        _ffn_kernel,
        grid=(tokens // FFN_TILE,),
        in_specs=[tok_spec, pl.BlockSpec((FFN_TILE, PLE_DIM), lambda t: (t, 0))]
        + [_resident(a.shape) for a in ffn_inputs[2:]],
        out_specs=tok_spec,
        out_shape=jax.ShapeDtypeStruct((tokens, D_MODEL), _F32),
        compiler_params=pltpu.CompilerParams(
            dimension_semantics=("arbitrary",),
            vmem_limit_bytes=VMEM_LIMIT_BYTES,
        ),
        name="channel_mixing",
    )(*ffn_inputs)
    return out.reshape(batch, seq, D_MODEL)
```

```python
import functools

import jax
import jax.numpy as jnp
from jax import lax
from jax.experimental import pallas as pl
from jax.experimental.pallas import tpu as pltpu

D_MODEL = 1024
PLE_DIM = 256
SG_GROUPS = 4
SG_CHUNK = 128
SG_WIDTH = D_MODEL // 2
SG_GROUP_DIM = SG_WIDTH // SG_GROUPS
GLA_HEADS = 4
GLA_DK = D_MODEL // 2
GLA_DV = D_MODEL
GLA_HEAD_K = GLA_DK // GLA_HEADS
GLA_HEAD_V = GLA_DV // GLA_HEADS
GLA_GATE_RANK = 16
GLA_GATE_TEMP = 16.0
GLA_CHUNK = 64
D_FF = 4 * D_MODEL
EPS = 1e-6

_C_U = 0
_C_Q = _C_U + 2 * SG_WIDTH
_C_LR = _C_Q + 2 * GLA_DK + GLA_DV
_C_OG = _C_LR + GLA_GATE_RANK
_C_BA = _C_OG + GLA_DV
_C_END = _C_BA + 2 * D_MODEL

MIX_TILE = 512
CUMSUM_ROWS = 256
PREP_STEPS = 7
BF16_SUBLANES = 16
FFN_TILE = 1024
FFN_SLAB = 1024
VMEM_LIMIT_BYTES = 56 * 1024 * 1024

_BF = jnp.bfloat16
_F32 = jnp.float32

_dot = functools.partial(jnp.dot, preferred_element_type=_F32)


def _rms(x, g):
    ms = jnp.mean(x * x, axis=-1, keepdims=True)
    return x * lax.rsqrt(ms + EPS) * g


def _gelu_tanh(x):
    c = 0.7978845608028654
    return x * (0.5 * (1.0 + jnp.tanh(c * (x + 0.044715 * (x * x * x)))))


def _sigmoid(x):
    return 0.5 * jnp.tanh(0.5 * x) + 0.5


def _log_sigmoid(x):
    return jnp.minimum(x, 0.0) - jnp.log1p(jnp.exp(-jnp.abs(x)))


def _cast_blocks(src_refs, dst_refs):
    for src, dst in zip(src_refs, dst_refs):
        dst[...] = src[...].astype(_BF)


def _prep_mix_weights_kernel(*refs):
    n = len(refs) // 2
    _cast_blocks(refs[:n], refs[n:])


def _mix_kernel(x_ref, g_ref, lmat_ref, w_in_t_ref, ln_g_ref, ln_b_ref, ws_ref, bs_ref,
                sg_wout_ref, wgu_ref, bgate_ref, gn_g_ref, gla_wout_ref, wo_ref,
                w_up_f, w_down_f, w_pg_f, w_pp_f,
                out_ref, w_up_o, w_down_o, w_pg_o, w_pp_o, state_ref, o_scr):
    ts = x_ref.shape[0]

    def proj(lo, hi):
        return lax.dot_general(hb, w_in_t_ref[lo:hi, :], (((1,), (1,)), ((), ())), preferred_element_type=_F32)

    @pl.when(pl.program_id(1) == 0)
    def _():
        state_ref[...] = jnp.zeros_like(state_ref)

    x = x_ref[...]
    hb = _rms(x, g_ref[...]).astype(_BF)

    glr = proj(_C_LR, _C_OG)
    uv = proj(_C_U, _C_Q)
    zg = _dot(glr.astype(_BF), wgu_ref[...]) + bgate_ref[...]
    qkv = proj(_C_Q, _C_LR)

    log_a = _log_sigmoid(zg) * (1.0 / GLA_GATE_TEMP)
    la_hi = log_a.astype(_BF)
    la_lo = (log_a - la_hi.astype(_F32)).astype(_BF)
    lmat = lmat_ref[...]
    b = jnp.concatenate(
        [_dot(lmat, la_hi[r0:r0 + CUMSUM_ROWS]) + _dot(lmat, la_lo[r0:r0 + CUMSUM_ROWS])
         for r0 in range(0, ts, CUMSUM_ROWS)], axis=0)

    u = _gelu_tanh(uv[:, :SG_WIDTH])
    v = _gelu_tanh(uv[:, SG_WIDTH:])
    mu = jnp.mean(v, axis=-1, keepdims=True)
    vc = v - mu
    var = jnp.mean(vc * vc, axis=-1, keepdims=True)
    vn = (vc * lax.rsqrt(var + EPS) * ln_g_ref[...] + ln_b_ref[...]).astype(_BF)

    og = proj(_C_OG, _C_BA)

    _cast_blocks((w_up_f, w_down_f, w_pg_f, w_pp_f), (w_up_o, w_down_o, w_pg_o, w_pp_o))

    q = qkv[:, :GLA_DK] * (GLA_HEAD_K ** -0.5)
    k = qkv[:, GLA_DK:2 * GLA_DK]
    vb = qkv[:, 2 * GLA_DK:].astype(_BF)
    r64 = lax.broadcasted_iota(jnp.int32, (GLA_CHUNK, GLA_CHUNK), 0)
    c64 = lax.broadcasted_iota(jnp.int32, (GLA_CHUNK, GLA_CHUNK), 1)
    causal = r64 >= c64
    rk = lax.broadcasted_iota(jnp.int32, (GLA_HEAD_K, GLA_HEAD_K), 0)
    ck = lax.broadcasted_iota(jnp.int32, (GLA_HEAD_K, GLA_HEAD_K), 1)
    eye_k = rk == ck
    n_chunks = ts // GLA_CHUNK
    scores_l, kv_l, qdec_l, dcol_l = {}, {}, {}, {}
    for c in range(n_chunks):
        r0 = c * GLA_CHUNK
        bc = b[r0:r0 + GLA_CHUNK]
        b_mid = bc[GLA_CHUNK // 2 - 1:GLA_CHUNK // 2]
        b_last = bc[GLA_CHUNK - 1:GLA_CHUNK]
        qc = q[r0:r0 + GLA_CHUNK]
        kc = k[r0:r0 + GLA_CHUNK]
        q_in = (qc * jnp.exp(bc - b_mid)).astype(_BF)
        k_in = (kc * jnp.exp(b_mid - bc)).astype(_BF)
        k_dec = (kc * jnp.exp(b_last - bc)).astype(_BF)
        qdec_l[c] = (qc * jnp.exp(bc)).astype(_BF)
        dec = jnp.exp(b_last)
        for hd in range(GLA_HEADS):
            ks = slice(hd * GLA_HEAD_K, (hd + 1) * GLA_HEAD_K)
            vs = slice(hd * GLA_HEAD_V, (hd + 1) * GLA_HEAD_V)
            sc = lax.dot_general(q_in[:, ks], k_in[:, ks], (((1,), (1,)), ((), ())),
                                 preferred_element_type=_F32)
            scores_l[c, hd] = jnp.where(causal, sc, 0.0).astype(_BF)
            kv_l[c, hd] = lax.dot_general(k_dec[:, ks], vb[r0:r0 + GLA_CHUNK, vs], (((0,), (0,)), ((), ())),
                                          preferred_element_type=_F32)
            dcol_l[c, hd] = jnp.sum(
                jnp.where(eye_k, jnp.broadcast_to(dec[:, ks], (GLA_HEAD_K, GLA_HEAD_K)), 0.0),
                axis=1, keepdims=True)

    bg = proj(_C_BA, _C_END)

    row = lax.broadcasted_iota(jnp.int32, (SG_CHUNK, SG_CHUNK), 0)
    col = lax.broadcasted_iota(jnp.int32, (SG_CHUNK, SG_CHUNK), 1)
    tril = row >= col
    w_sp = [jnp.where(tril, ws_ref[g], 0.0).astype(_BF) for g in range(SG_GROUPS)]
    bias = bs_ref[...]
    sg_chunks = ts // SG_CHUNK
    per_group = []
    for g in range(SG_GROUPS):
        gs = slice(g * SG_GROUP_DIM, (g + 1) * SG_GROUP_DIM)
        v_wide = jnp.concatenate([vn[c * SG_CHUNK:(c + 1) * SG_CHUNK, gs] for c in range(sg_chunks)], axis=1)
        per_group.append(_dot(w_sp[g], v_wide))
    mixed = jnp.concatenate(
        [jnp.concatenate([per_group[g][:, c * SG_GROUP_DIM:(c + 1) * SG_GROUP_DIM] for g in range(SG_GROUPS)], axis=1)
         + bias for c in range(sg_chunks)], axis=0)
    z = (u * mixed).astype(_BF)
    y_a = _dot(z, sg_wout_ref[...])

    sprev_l = {}
    for hd in range(GLA_HEADS):
        s_run = state_ref[hd]
        for c in range(n_chunks):
            sprev_l[c, hd] = s_run.astype(_BF)
            s_run = dcol_l[c, hd] * s_run + kv_l[c, hd]
        state_ref[hd] = s_run

    for c in range(n_chunks):
        r0 = c * GLA_CHUNK
        for hd in range(GLA_HEADS):
            ks = slice(hd * GLA_HEAD_K, (hd + 1) * GLA_HEAD_K)
            vs = slice(hd * GLA_HEAD_V, (hd + 1) * GLA_HEAD_V)
            lhs = jnp.concatenate([qdec_l[c][:, ks], scores_l[c, hd]], axis=1)
            rhs = jnp.concatenate([sprev_l[c, hd], vb[r0:r0 + GLA_CHUNK, vs]], axis=0)
            o_scr[r0:r0 + GLA_CHUNK, vs] = _dot(lhs, rhs)
    gn_g = gn_g_ref[...]
    o_heads = []
    for hd in range(GLA_HEADS):
        oh = o_scr[:, hd * GLA_HEAD_V:(hd + 1) * GLA_HEAD_V]
        o_heads.append(_rms(oh, gn_g))
    o_n = jnp.concatenate(o_heads, axis=1)
    o_g = (o_n * (og * _sigmoid(og))).astype(_BF)

    half = ts // 2
    y_b = [_dot(o_g[r0:r0 + half], gla_wout_ref[...]) for r0 in (0, half)]
    for j, r0 in enumerate((0, half)):
        merged = (_sigmoid(bg[r0:r0 + half, :D_MODEL]) * y_a[r0:r0 + half]
                  + _sigmoid(bg[r0:r0 + half, D_MODEL:]) * y_b[j])
        out_ref[r0:r0 + half, :] = x[r0:r0 + half] + _dot(merged.astype(_BF), wo_ref[...])


def _ffn_kernel(x_ref, p_ref, g_ffn_ref, w_up_ref, w_down_ref, g_ple_ref, w_pg_ref, w_pp_ref,
                g_fin_ref, out_ref):
    x = x_ref[...]
    hb = _rms(x, g_ffn_ref[...]).astype(_BF)
    ple = _dot(p_ref[...].astype(_BF), w_pp_ref[...])
    acc = x
    for f0 in range(0, D_FF, FFN_SLAB):
        up = jnp.maximum(_dot(hb, w_up_ref[:, f0:f0 + FFN_SLAB]), 0.0)
        acc = acc + _dot((up * up).astype(_BF), w_down_ref[f0:f0 + FFN_SLAB, :])
    half = x_ref.shape[0] // 2
    for r0 in (0, half):
        acc_h = acc[r0:r0 + half]
        h3 = _rms(acc_h, g_ple_ref[...]).astype(_BF)
        gate = _sigmoid(_dot(h3, w_pg_ref[...]))
        out_ref[r0:r0 + half, :] = _rms(acc_h + gate * ple[r0:r0 + half], g_fin_ref[...])


def _resident(shape):
    zeros = (0,) * len(shape)
    return pl.BlockSpec(shape, lambda *_: zeros, pipeline_mode=pl.Buffered(1))


def _slab_specs(arr, steps, step_of):
    rows = arr.shape[1]
    n = max(k for k in range(1, steps + 1) if rows % k == 0 and (rows // k) % BF16_SUBLANES == 0)
    in_spec = pl.BlockSpec((None, rows // n, arr.shape[2]), lambda *g: (0, jnp.minimum(step_of(*g), n - 1), 0))
    out_spec = pl.BlockSpec((rows // n, arr.shape[2]), lambda *g: (jnp.minimum(step_of(*g), n - 1), 0))
    return in_spec, out_spec


def _bf16_like(arr):
    return jax.ShapeDtypeStruct(arr.shape[1:], _BF)


def kernel(x, p, norm_mix_g, w_in, sg_ln_g, sg_ln_b, sg_w_s, sg_b_s, sg_w_out, gla_w_gate_up, gla_b_gate, gla_norm_g, gla_w_out, w_o, norm_ffn_g, ffn_w_up, ffn_w_down, ple_norm_g, ple_w_gate, ple_w_proj, final_norm_g):
    batch, seq, d = x.shape
    assert w_in.shape[0] == 1
    assert d == D_MODEL and seq % MIX_TILE == 0 and (batch * seq) % FFN_TILE == 0
    tokens = batch * seq
    i = 0

    mix_weights = (jnp.swapaxes(w_in, 1, 2), sg_w_out, gla_w_gate_up, gla_w_out, w_o)
    prep_specs = [_slab_specs(a, PREP_STEPS, lambda r: r) for a in mix_weights]
    w_in_t, sg_wout_b, wgu_b, gla_wout_b, wo_b = pl.pallas_call(
        _prep_mix_weights_kernel,
        grid=(PREP_STEPS,),
        in_specs=[sp[0] for sp in prep_specs],
        out_specs=[sp[1] for sp in prep_specs],
        out_shape=[_bf16_like(a) for a in mix_weights],
        compiler_params=pltpu.CompilerParams(dimension_semantics=("arbitrary",)),
        name="mix_weight_prep",
    )(*mix_weights)

    bias_sp = jnp.repeat(sg_b_s[i].T, SG_GROUP_DIM, axis=1)
    pos = jnp.arange(CUMSUM_ROWS)
    lmat = ((pos[:, None] // GLA_CHUNK == pos[None, :] // GLA_CHUNK) & (pos[None, :] <= pos[:, None])).astype(_BF)

    resident_inputs = (
        norm_mix_g[i].reshape(1, D_MODEL),
        lmat,
        w_in_t,
        sg_ln_g[i].reshape(1, SG_WIDTH),
        sg_ln_b[i].reshape(1, SG_WIDTH),
        sg_w_s[i],
        bias_sp,
        sg_wout_b,
        wgu_b,
        gla_b_gate[i].reshape(1, GLA_DK),
        gla_norm_g[i].reshape(1, GLA_HEAD_V),
        gla_wout_b,
        wo_b,
    )
    n_seq = seq // MIX_TILE
    side_weights = (ffn_w_up, ffn_w_down, ple_w_gate, ple_w_proj)
    side_specs = [_slab_specs(a, batch * n_seq, lambda bi, si: bi * n_seq + si) for a in side_weights]
    x_spec = pl.BlockSpec((None, MIX_TILE, D_MODEL), lambda bi, si: (bi, si, 0))
    x, w_up_b, w_down_b, w_pg_b, w_pp_b = pl.pallas_call(
        _mix_kernel,
        grid=(batch, n_seq),
        in_specs=[x_spec] + [_resident(a.shape) for a in resident_inputs] + [sp[0] for sp in side_specs],
        out_specs=[x_spec] + [sp[1] for sp in side_specs],
        out_shape=[jax.ShapeDtypeStruct((batch, seq, D_MODEL), _F32)] + [_bf16_like(a) for a in side_weights],
        scratch_shapes=[
            pltpu.VMEM((GLA_HEADS, GLA_HEAD_K, GLA_HEAD_V), _F32),
            pltpu.VMEM((MIX_TILE, GLA_DV), _F32),
        ],
        compiler_params=pltpu.CompilerParams(
            dimension_semantics=("arbitrary", "arbitrary"),
            vmem_limit_bytes=VMEM_LIMIT_BYTES,
        ),
        name="token_mixing",
    )(x, *resident_inputs, *side_weights)

    ffn_inputs = (
        x.reshape(tokens, D_MODEL),
        p[i].reshape(tokens, PLE_DIM),
        norm_ffn_g[i].reshape(1, D_MODEL),
        w_up_b,
        w_down_b,
        ple_norm_g[i].reshape(1, D_MODEL),
        w_pg_b,
        w_pp_b,
        final_norm_g.reshape(1, D_MODEL),
    )
    tok_spec = pl.BlockSpec((FFN_TILE, D_MODEL), lambda t: (t, 0))
    out = pl.pallas_call(
        _ffn_kernel,
        grid=(tokens // FFN_TILE,),
        in_specs=[tok_spec, pl.BlockSpec((FFN_TILE, PLE_DIM), lambda t: (t, 0))]
        + [_resident(a.shape) for a in ffn_inputs[2:]],
        out_specs=tok_spec,
        out_shape=jax.ShapeDtypeStruct((tokens, D_MODEL), _F32),
        compiler_params=pltpu.CompilerParams(
            dimension_semantics=("arbitrary",),
            vmem_limit_bytes=VMEM_LIMIT_BYTES,
        ),
        name="channel_mixing",
    )(*ffn_inputs)
    return out.reshape(batch, seq, D_MODEL)
```

```python
import functools

import jax
import jax.numpy as jnp
from jax import lax
from jax.experimental import pallas as pl
from jax.experimental.pallas import tpu as pltpu

D_MODEL = 1024
PLE_DIM = 256
SG_GROUPS = 4
SG_CHUNK = 128
SG_WIDTH = D_MODEL // 2
SG_GROUP_DIM = SG_WIDTH // SG_GROUPS
GLA_HEADS = 4
GLA_DK = D_MODEL // 2
GLA_DV = D_MODEL
GLA_HEAD_K = GLA_DK // GLA_HEADS
GLA_HEAD_V = GLA_DV // GLA_HEADS
GLA_GATE_RANK = 16
GLA_GATE_TEMP = 16.0
GLA_CHUNK = 64
D_FF = 4 * D_MODEL
EPS = 1e-6

_C_U = 0
_C_Q = _C_U + 2 * SG_WIDTH
_C_LR = _C_Q + 2 * GLA_DK + GLA_DV
_C_OG = _C_LR + GLA_GATE_RANK
_C_BA = _C_OG + GLA_DV
_C_END = _C_BA + 2 * D_MODEL

MIX_TILE = 512
CUMSUM_ROWS = 256
MIX_PREP_STEPS = 11
BF16_SUBLANES = 16
FFN_TILE = 1024
FFN_SLAB = 1024
VMEM_LIMIT_BYTES = 56 * 1024 * 1024

_BF = jnp.bfloat16
_F32 = jnp.float32

_dot = functools.partial(jnp.dot, preferred_element_type=_F32)


def _rms(x, g):
    ms = jnp.mean(x * x, axis=-1, keepdims=True)
    return x * lax.rsqrt(ms + EPS) * g


def _gelu_tanh(x):
    c = 0.7978845608028654
    return x * (0.5 * (1.0 + jnp.tanh(c * (x + 0.044715 * (x * x * x)))))


def _sigmoid(x):
    return 0.5 * jnp.tanh(0.5 * x) + 0.5


def _log_sigmoid(x):
    return jnp.minimum(x, 0.0) - jnp.log1p(jnp.exp(-jnp.abs(x)))


def _cast_blocks(src_refs, dst_refs):
    for src, dst in zip(src_refs, dst_refs):
        dst[...] = src[...].astype(_BF)


def _mix_kernel(x_ref, g_ref, lmat_ref, w_in_t_f, sg_wout_f, wgu_f, gla_wout_f, wo_f,
                ln_g_ref, ln_b_ref, ws_ref, bs_ref, bgate_ref, gn_g_ref,
                w_up_f, w_down_f, w_pg_f, w_pp_f,
                out_ref, w_up_o, w_down_o, w_pg_o, w_pp_o,
                state_ref, o_scr, w_in_t_ref, sg_wout_ref, wgu_ref, gla_wout_ref, wo_ref, *, n_seq, prep_steps):
    step = pl.program_id(0)

    @pl.when(step < prep_steps)
    def _():
        for src, dst in ((w_in_t_f, w_in_t_ref), (sg_wout_f, sg_wout_ref), (wgu_f, wgu_ref),
                         (gla_wout_f, gla_wout_ref), (wo_f, wo_ref)):
            rows = src.shape[0]
            n_slabs = dst.shape[0] // rows
            if n_slabs == 1:
                dst[...] = src[...].astype(_BF)
            else:
                r0 = pl.multiple_of(jnp.minimum(step, n_slabs - 1) * rows, rows)
                dst[pl.ds(r0, rows), :] = src[...].astype(_BF)

    @pl.when(step >= prep_steps)
    def _():
        _mix_tile((step - prep_steps) % n_seq == 0,
                  x_ref, g_ref, lmat_ref, w_in_t_ref, ln_g_ref, ln_b_ref, ws_ref, bs_ref,
                  sg_wout_ref, wgu_ref, bgate_ref, gn_g_ref, gla_wout_ref, wo_ref,
                  w_up_f, w_down_f, w_pg_f, w_pp_f,
                  out_ref, w_up_o, w_down_o, w_pg_o, w_pp_o, state_ref, o_scr)


def _mix_tile(first_of_sequence, x_ref, g_ref, lmat_ref, w_in_t_ref, ln_g_ref, ln_b_ref, ws_ref, bs_ref,
              sg_wout_ref, wgu_ref, bgate_ref, gn_g_ref, gla_wout_ref, wo_ref,
              w_up_f, w_down_f, w_pg_f, w_pp_f,
              out_ref, w_up_o, w_down_o, w_pg_o, w_pp_o, state_ref, o_scr):
    ts = x_ref.shape[0]

    def proj(lo, hi):
        return lax.dot_general(hb, w_in_t_ref[lo:hi, :], (((1,), (1,)), ((), ())), preferred_element_type=_F32)

    @pl.when(first_of_sequence)
    def _():
        state_ref[...] = jnp.zeros_like(state_ref)

    x = x_ref[...]
    hb = _rms(x, g_ref[...]).astype(_BF)

    glr = proj(_C_LR, _C_OG)
    uv = proj(_C_U, _C_Q)
    zg = _dot(glr.astype(_BF), wgu_ref[...]) + bgate_ref[...]
    qkv = proj(_C_Q, _C_LR)

    log_a = _log_sigmoid(zg) * (1.0 / GLA_GATE_TEMP)
    la_hi = log_a.astype(_BF)
    la_lo = (log_a - la_hi.astype(_F32)).astype(_BF)
    lmat = lmat_ref[...]
    b = jnp.concatenate(
        [_dot(lmat, la_hi[r0:r0 + CUMSUM_ROWS]) + _dot(lmat, la_lo[r0:r0 + CUMSUM_ROWS])
         for r0 in range(0, ts, CUMSUM_ROWS)], axis=0)

    u = _gelu_tanh(uv[:, :SG_WIDTH])
    v = _gelu_tanh(uv[:, SG_WIDTH:])
    mu = jnp.mean(v, axis=-1, keepdims=True)
    vc = v - mu
    var = jnp.mean(vc * vc, axis=-1, keepdims=True)
    vn = (vc * lax.rsqrt(var + EPS) * ln_g_ref[...] + ln_b_ref[...]).astype(_BF)

    og = proj(_C_OG, _C_BA)

    _cast_blocks((w_up_f, w_down_f, w_pg_f, w_pp_f), (w_up_o, w_down_o, w_pg_o, w_pp_o))

    q = qkv[:, :GLA_DK] * (GLA_HEAD_K ** -0.5)
    k = qkv[:, GLA_DK:2 * GLA_DK]
    vb = qkv[:, 2 * GLA_DK:].astype(_BF)
    r64 = lax.broadcasted_iota(jnp.int32, (GLA_CHUNK, GLA_CHUNK), 0)
    c64 = lax.broadcasted_iota(jnp.int32, (GLA_CHUNK, GLA_CHUNK), 1)
    causal = r64 >= c64
    rk = lax.broadcasted_iota(jnp.int32, (GLA_HEAD_K, GLA_HEAD_K), 0)
    ck = lax.broadcasted_iota(jnp.int32, (GLA_HEAD_K, GLA_HEAD_K), 1)
    eye_k = rk == ck
    n_chunks = ts // GLA_CHUNK
    scores_l, kv_l, qdec_l, dcol_l = {}, {}, {}, {}
    for c in range(n_chunks):
        r0 = c * GLA_CHUNK
        bc = b[r0:r0 + GLA_CHUNK]
        b_mid = bc[GLA_CHUNK // 2 - 1:GLA_CHUNK // 2]
        b_last = bc[GLA_CHUNK - 1:GLA_CHUNK]
        qc = q[r0:r0 + GLA_CHUNK]
        kc = k[r0:r0 + GLA_CHUNK]
        q_in = (qc * jnp.exp(bc - b_mid)).astype(_BF)
        k_in = (kc * jnp.exp(b_mid - bc)).astype(_BF)
        k_dec = (kc * jnp.exp(b_last - bc)).astype(_BF)
        qdec_l[c] = (qc * jnp.exp(bc)).astype(_BF)
        dec = jnp.exp(b_last)
        for hd in range(GLA_HEADS):
            ks = slice(hd * GLA_HEAD_K, (hd + 1) * GLA_HEAD_K)
            vs = slice(hd * GLA_HEAD_V, (hd + 1) * GLA_HEAD_V)
            sc = lax.dot_general(q_in[:, ks], k_in[:, ks], (((1,), (1,)), ((), ())),
                                 preferred_element_type=_F32)
            scores_l[c, hd] = jnp.where(causal, sc, 0.0).astype(_BF)
            kv_l[c, hd] = lax.dot_general(k_dec[:, ks], vb[r0:r0 + GLA_CHUNK, vs], (((0,), (0,)), ((), ())),
                                          preferred_element_type=_F32)
            dcol_l[c, hd] = jnp.sum(
                jnp.where(eye_k, jnp.broadcast_to(dec[:, ks], (GLA_HEAD_K, GLA_HEAD_K)), 0.0),
                axis=1, keepdims=True)

    bg = proj(_C_BA, _C_END)

    row = lax.broadcasted_iota(jnp.int32, (SG_CHUNK, SG_CHUNK), 0)
    col = lax.broadcasted_iota(jnp.int32, (SG_CHUNK, SG_CHUNK), 1)
    tril = row >= col
    w_sp = [jnp.where(tril, ws_ref[g], 0.0).astype(_BF) for g in range(SG_GROUPS)]
    bias = bs_ref[...]
    sg_chunks = ts // SG_CHUNK
    per_group = []
    for g in range(SG_GROUPS):
        gs = slice(g * SG_GROUP_DIM, (g + 1) * SG_GROUP_DIM)
        v_wide = jnp.concatenate([vn[c * SG_CHUNK:(c + 1) * SG_CHUNK, gs] for c in range(sg_chunks)], axis=1)
        per_group.append(_dot(w_sp[g], v_wide))
    mixed = jnp.concatenate(
        [jnp.concatenate([per_group[g][:, c * SG_GROUP_DIM:(c + 1) * SG_GROUP_DIM] for g in range(SG_GROUPS)], axis=1)
         + bias for c in range(sg_chunks)], axis=0)
    z = (u * mixed).astype(_BF)
    y_a = _dot(z, sg_wout_ref[...])

    sprev_l = {}
    for hd in range(GLA_HEADS):
        s_run = state_ref[hd]
        for c in range(n_chunks):
            sprev_l[c, hd] = s_run.astype(_BF)
            s_run = dcol_l[c, hd] * s_run + kv_l[c, hd]
        state_ref[hd] = s_run

    for c in range(n_chunks):
        r0 = c * GLA_CHUNK
        for hd in range(GLA_HEADS):
            ks = slice(hd * GLA_HEAD_K, (hd + 1) * GLA_HEAD_K)
            vs = slice(hd * GLA_HEAD_V, (hd + 1) * GLA_HEAD_V)
            lhs = jnp.concatenate([qdec_l[c][:, ks], scores_l[c, hd]], axis=1)
            rhs = jnp.concatenate([sprev_l[c, hd], vb[r0:r0 + GLA_CHUNK, vs]], axis=0)
            o_scr[r0:r0 + GLA_CHUNK, vs] = _dot(lhs, rhs)
    gn_g = gn_g_ref[...]
    o_heads = []
    for hd in range(GLA_HEADS):
        oh = o_scr[:, hd * GLA_HEAD_V:(hd + 1) * GLA_HEAD_V]
        o_heads.append(_rms(oh, gn_g))
    o_n = jnp.concatenate(o_heads, axis=1)
    o_g = (o_n * (og * _sigmoid(og))).astype(_BF)

    half = ts // 2
    y_b = [_dot(o_g[r0:r0 + half], gla_wout_ref[...]) for r0 in (0, half)]
    for j, r0 in enumerate((0, half)):
        merged = (_sigmoid(bg[r0:r0 + half, :D_MODEL]) * y_a[r0:r0 + half]
                  + _sigmoid(bg[r0:r0 + half, D_MODEL:]) * y_b[j])
        out_ref[r0:r0 + half, :] = x[r0:r0 + half] + _dot(merged.astype(_BF), wo_ref[...])


def _ffn_kernel(x_ref, p_ref, g_ffn_ref, w_up_ref, w_down_ref, g_ple_ref, w_pg_ref, w_pp_ref,
                g_fin_ref, out_ref):
    x = x_ref[...]
    hb = _rms(x, g_ffn_ref[...]).astype(_BF)
    ple = _dot(p_ref[...].astype(_BF), w_pp_ref[...])
    acc = x
    for f0 in range(0, D_FF, FFN_SLAB):
        up = jnp.maximum(_dot(hb, w_up_ref[:, f0:f0 + FFN_SLAB]), 0.0)
        acc = acc + _dot((up * up).astype(_BF), w_down_ref[f0:f0 + FFN_SLAB, :])
    half = x_ref.shape[0] // 2
    for r0 in (0, half):
        acc_h = acc[r0:r0 + half]
        h3 = _rms(acc_h, g_ple_ref[...]).astype(_BF)
        gate = _sigmoid(_dot(h3, w_pg_ref[...]))
        out_ref[r0:r0 + half, :] = _rms(acc_h + gate * ple[r0:r0 + half], g_fin_ref[...])


def _resident(shape):
    zeros = (0,) * len(shape)
    return pl.BlockSpec(shape, lambda *_: zeros, pipeline_mode=pl.Buffered(1))


def _slab_specs(arr, steps, step_of):
    rows = arr.shape[1]
    n = max(k for k in range(1, steps + 1) if rows % k == 0 and (rows // k) % BF16_SUBLANES == 0)
    in_spec = pl.BlockSpec((None, rows // n, arr.shape[2]), lambda *g: (0, jnp.minimum(step_of(*g), n - 1), 0))
    out_spec = pl.BlockSpec((rows // n, arr.shape[2]), lambda *g: (jnp.minimum(step_of(*g), n - 1), 0))
    return in_spec, out_spec


def _bf16_like(arr):
    return jax.ShapeDtypeStruct(arr.shape[1:], _BF)


def kernel(x, p, norm_mix_g, w_in, sg_ln_g, sg_ln_b, sg_w_s, sg_b_s, sg_w_out, gla_w_gate_up, gla_b_gate, gla_norm_g, gla_w_out, w_o, norm_ffn_g, ffn_w_up, ffn_w_down, ple_norm_g, ple_w_gate, ple_w_proj, final_norm_g):
    batch, seq, d = x.shape
    assert w_in.shape[0] == 1
    assert d == D_MODEL and seq % MIX_TILE == 0 and (batch * seq) % FFN_TILE == 0
    tokens = batch * seq
    i = 0

    mix_weights = (jnp.swapaxes(w_in, 1, 2), sg_w_out, gla_w_gate_up, gla_w_out, w_o)
    side_weights = (ffn_w_up, ffn_w_down, ple_w_gate, ple_w_proj)

    bias_sp = jnp.repeat(sg_b_s[i].T, SG_GROUP_DIM, axis=1)
    pos = jnp.arange(CUMSUM_ROWS)
    lmat = ((pos[:, None] // GLA_CHUNK == pos[None, :] // GLA_CHUNK) & (pos[None, :] <= pos[:, None])).astype(_BF)

    n_seq = seq // MIX_TILE
    n_tiles = batch * n_seq

    def tile_of(step):
        return jnp.maximum(step - MIX_PREP_STEPS, 0)

    x_spec = pl.BlockSpec((None, MIX_TILE, D_MODEL), lambda s: (tile_of(s) // n_seq, tile_of(s) % n_seq, 0))
    prep_specs = [_slab_specs(a, MIX_PREP_STEPS, lambda s: s) for a in mix_weights]
    side_specs = [_slab_specs(a, n_tiles, tile_of) for a in side_weights]
    pre = (norm_mix_g[i].reshape(1, D_MODEL), lmat)
    post = (sg_ln_g[i].reshape(1, SG_WIDTH), sg_ln_b[i].reshape(1, SG_WIDTH), sg_w_s[i], bias_sp,
            gla_b_gate[i].reshape(1, GLA_DK), gla_norm_g[i].reshape(1, GLA_HEAD_V))
    x, w_up_b, w_down_b, w_pg_b, w_pp_b = pl.pallas_call(
        functools.partial(_mix_kernel, n_seq=n_seq, prep_steps=MIX_PREP_STEPS),
        grid=(MIX_PREP_STEPS + n_tiles,),
        in_specs=[x_spec] + [_resident(a.shape) for a in pre] + [sp[0] for sp in prep_specs]
        + [_resident(a.shape) for a in post] + [sp[0] for sp in side_specs],
        out_specs=[x_spec] + [sp[1] for sp in side_specs],
        out_shape=[jax.ShapeDtypeStruct((batch, seq, D_MODEL), _F32)] + [_bf16_like(a) for a in side_weights],
        scratch_shapes=[
            pltpu.VMEM((GLA_HEADS, GLA_HEAD_K, GLA_HEAD_V), _F32),
            pltpu.VMEM((MIX_TILE, GLA_DV), _F32),
        ] + [pltpu.VMEM(a.shape[1:], _BF) for a in mix_weights],
        compiler_params=pltpu.CompilerParams(
            dimension_semantics=("arbitrary",),
            vmem_limit_bytes=VMEM_LIMIT_BYTES,
        ),
        name="token_mixing",
    )(x, *pre, *mix_weights, *post, *side_weights)

    ffn_inputs = (
        x.reshape(tokens, D_MODEL),
        p[i].reshape(tokens, PLE_DIM),
        norm_ffn_g[i].reshape(1, D_MODEL),
        w_up_b,
        w_down_b,
        ple_norm_g[i].reshape(1, D_MODEL),
        w_pg_b,
        w_pp_b,
        final_norm_g.reshape(1, D_MODEL),
    )
    tok_spec = pl.BlockSpec((FFN_TILE, D_MODEL), lambda t: (t, 0))
    out = pl.pallas_call(
        _ffn_kernel,
        grid=(tokens // FFN_TILE,),
        in_specs=[tok_spec, pl.BlockSpec((FFN_TILE, PLE_DIM), lambda t: (t, 0))]
        + [_resident(a.shape) for a in ffn_inputs[2:]],
        out_specs=tok_spec,
        out_shape=jax.ShapeDtypeStruct((tokens, D_MODEL), _F32),
        compiler_params=pltpu.CompilerParams(
            dimension_semantics=("arbitrary",),
            vmem_limit_bytes=VMEM_LIMIT_BYTES,
        ),
        name="channel_mixing",
    )(*ffn_inputs)
    return out.reshape(batch, seq, D_MODEL)
```

```python
import functools

import jax
import jax.numpy as jnp
from jax import lax
from jax.experimental import pallas as pl
from jax.experimental.pallas import tpu as pltpu

D_MODEL = 1024
PLE_DIM = 256
SG_GROUPS = 4
SG_CHUNK = 128
SG_WIDTH = D_MODEL // 2
SG_GROUP_DIM = SG_WIDTH // SG_GROUPS
GLA_HEADS = 4
GLA_DK = D_MODEL // 2
GLA_DV = D_MODEL
GLA_HEAD_K = GLA_DK // GLA_HEADS
GLA_HEAD_V = GLA_DV // GLA_HEADS
GLA_GATE_RANK = 16
GLA_GATE_TEMP = 16.0
GLA_CHUNK = 64
D_FF = 4 * D_MODEL
EPS = 1e-6

_C_U = 0
_C_Q = _C_U + 2 * SG_WIDTH
_C_LR = _C_Q + 2 * GLA_DK + GLA_DV
_C_OG = _C_LR + GLA_GATE_RANK
_C_BA = _C_OG + GLA_DV
_C_END = _C_BA + 2 * D_MODEL

MIX_TILE = 512
CUMSUM_ROWS = 256
MIX_PREP_STEPS = 11
BF16_SUBLANES = 16
FFN_TILE = 1024
FFN_SLAB = 1024
VMEM_LIMIT_BYTES = 56 * 1024 * 1024
MIX_VMEM_LIMIT_BYTES = 62 * 1024 * 1024

_BF = jnp.bfloat16
_F32 = jnp.float32

_dot = functools.partial(jnp.dot, preferred_element_type=_F32)


def _rms(x, g):
    ms = jnp.mean(x * x, axis=-1, keepdims=True)
    return x * lax.rsqrt(ms + EPS) * g


def _gelu_tanh(x):
    c = 0.7978845608028654
    return x * (0.5 * (1.0 + jnp.tanh(c * (x + 0.044715 * (x * x * x)))))


def _sigmoid(x):
    return 0.5 * jnp.tanh(0.5 * x) + 0.5


def _log_sigmoid(x):
    return jnp.minimum(x, 0.0) - jnp.log1p(jnp.exp(-jnp.abs(x)))


def _cast_blocks(src_refs, dst_refs):
    for src, dst in zip(src_refs, dst_refs):
        dst[...] = src[...].astype(_BF)


def _mix_kernel(x_ref, xnext_ref, g_ref, lmat_ref, w_in_t_f, sg_wout_f, wgu_f, gla_wout_f, wo_f,
                ln_g_ref, ln_b_ref, ws_ref, bs_ref, bgate_ref, gn_g_ref,
                w_up_f, w_down_f, w_pg_f, w_pp_f,
                out_ref, w_up_o, w_down_o, w_pg_o, w_pp_o,
                state_ref, o_scr, w_in_t_ref, sg_wout_ref, wgu_ref, gla_wout_ref, wo_ref, hb_scr,
                *, n_seq, prep_steps):
    step = pl.program_id(0)

    @pl.when(step < prep_steps)
    def _():
        for src, dst in ((w_in_t_f, w_in_t_ref), (sg_wout_f, sg_wout_ref), (wgu_f, wgu_ref),
                         (gla_wout_f, gla_wout_ref), (wo_f, wo_ref)):
            rows = src.shape[0]
            n_slabs = dst.shape[0] // rows
            if n_slabs == 1:
                dst[...] = src[...].astype(_BF)
            else:
                r0 = pl.multiple_of(jnp.minimum(step, n_slabs - 1) * rows, rows)
                dst[pl.ds(r0, rows), :] = src[...].astype(_BF)

    @pl.when(step == prep_steps - 1)
    def _():
        hb_scr[0] = _rms(x_ref[...], g_ref[...]).astype(_BF)

    @pl.when(step >= prep_steps)
    def _():
        _mix_tile((step - prep_steps) % n_seq == 0, (step - prep_steps) % 2, hb_scr, xnext_ref,
                  x_ref, g_ref, lmat_ref, w_in_t_ref, ln_g_ref, ln_b_ref, ws_ref, bs_ref,
                  sg_wout_ref, wgu_ref, bgate_ref, gn_g_ref, gla_wout_ref, wo_ref,
                  w_up_f, w_down_f, w_pg_f, w_pp_f,
                  out_ref, w_up_o, w_down_o, w_pg_o, w_pp_o, state_ref, o_scr)


def _mix_tile(first_of_sequence, slot, hb_scr, xnext_ref, x_ref, g_ref, lmat_ref, w_in_t_ref, ln_g_ref, ln_b_ref, ws_ref, bs_ref,
              sg_wout_ref, wgu_ref, bgate_ref, gn_g_ref, gla_wout_ref, wo_ref,
              w_up_f, w_down_f, w_pg_f, w_pp_f,
              out_ref, w_up_o, w_down_o, w_pg_o, w_pp_o, state_ref, o_scr):
    ts = x_ref.shape[0]

    def proj(lo, hi):
        return lax.dot_general(hb, w_in_t_ref[lo:hi, :], (((1,), (1,)), ((), ())), preferred_element_type=_F32)

    @pl.when(first_of_sequence)
    def _():
        state_ref[...] = jnp.zeros_like(state_ref)

    x = x_ref[...]
    hb = hb_scr[slot]
    hb_scr[1 - slot] = _rms(xnext_ref[...], g_ref[...]).astype(_BF)

    glr = proj(_C_LR, _C_OG)
    uv = proj(_C_U, _C_Q)
    zg = _dot(glr.astype(_BF), wgu_ref[...]) + bgate_ref[...]
    qkv = proj(_C_Q, _C_LR)

    log_a = _log_sigmoid(zg) * (1.0 / GLA_GATE_TEMP)
    la_hi = log_a.astype(_BF)
    la_lo = (log_a - la_hi.astype(_F32)).astype(_BF)
    lmat = lmat_ref[...]
    b = jnp.concatenate(
        [_dot(lmat, la_hi[r0:r0 + CUMSUM_ROWS]) + _dot(lmat, la_lo[r0:r0 + CUMSUM_ROWS])
         for r0 in range(0, ts, CUMSUM_ROWS)], axis=0)

    u = _gelu_tanh(uv[:, :SG_WIDTH])
    v = _gelu_tanh(uv[:, SG_WIDTH:])
    mu = jnp.mean(v, axis=-1, keepdims=True)
    vc = v - mu
    var = jnp.mean(vc * vc, axis=-1, keepdims=True)
    vn = (vc * lax.rsqrt(var + EPS) * ln_g_ref[...] + ln_b_ref[...]).astype(_BF)

    og = proj(_C_OG, _C_BA)

    _cast_blocks((w_up_f, w_down_f, w_pg_f, w_pp_f), (w_up_o, w_down_o, w_pg_o, w_pp_o))

    q = qkv[:, :GLA_DK] * (GLA_HEAD_K ** -0.5)
    k = qkv[:, GLA_DK:2 * GLA_DK]
    vb = qkv[:, 2 * GLA_DK:].astype(_BF)
    r64 = lax.broadcasted_iota(jnp.int32, (GLA_CHUNK, GLA_CHUNK), 0)
    c64 = lax.broadcasted_iota(jnp.int32, (GLA_CHUNK, GLA_CHUNK), 1)
    causal = r64 >= c64
    rk = lax.broadcasted_iota(jnp.int32, (GLA_HEAD_K, GLA_HEAD_K), 0)
    ck = lax.broadcasted_iota(jnp.int32, (GLA_HEAD_K, GLA_HEAD_K), 1)
    eye_k = rk == ck
    n_chunks = ts // GLA_CHUNK
    scores_l, kv_l, qdec_l, dcol_l = {}, {}, {}, {}
    for c in range(n_chunks):
        r0 = c * GLA_CHUNK
        bc = b[r0:r0 + GLA_CHUNK]
        b_mid = bc[GLA_CHUNK // 2 - 1:GLA_CHUNK // 2]
        b_last = bc[GLA_CHUNK - 1:GLA_CHUNK]
        qc = q[r0:r0 + GLA_CHUNK]
        kc = k[r0:r0 + GLA_CHUNK]
        q_in = (qc * jnp.exp(bc - b_mid)).astype(_BF)
        k_in = (kc * jnp.exp(b_mid - bc)).astype(_BF)
        k_dec = (kc * jnp.exp(b_last - bc)).astype(_BF)
        qdec_l[c] = (qc * jnp.exp(bc)).astype(_BF)
        dec = jnp.exp(b_last)
        for hd in range(GLA_HEADS):
            ks = slice(hd * GLA_HEAD_K, (hd + 1) * GLA_HEAD_K)
            vs = slice(hd * GLA_HEAD_V, (hd + 1) * GLA_HEAD_V)
            sc = lax.dot_general(q_in[:, ks], k_in[:, ks], (((1,), (1,)), ((), ())),
                                 preferred_element_type=_F32)
            scores_l[c, hd] = jnp.where(causal, sc, 0.0).astype(_BF)
            kv_l[c, hd] = lax.dot_general(k_dec[:, ks], vb[r0:r0 + GLA_CHUNK, vs], (((0,), (0,)), ((), ())),
                                          preferred_element_type=_F32)
            dcol_l[c, hd] = jnp.sum(
                jnp.where(eye_k, jnp.broadcast_to(dec[:, ks], (GLA_HEAD_K, GLA_HEAD_K)), 0.0),
                axis=1, keepdims=True)

    bg = proj(_C_BA, _C_END)

    row = lax.broadcasted_iota(jnp.int32, (SG_CHUNK, SG_CHUNK), 0)
    col = lax.broadcasted_iota(jnp.int32, (SG_CHUNK, SG_CHUNK), 1)
    tril = row >= col
    w_sp = [jnp.where(tril, ws_ref[g], 0.0).astype(_BF) for g in range(SG_GROUPS)]
    bias = bs_ref[...]
    sg_chunks = ts // SG_CHUNK
    per_group = []
    for g in range(SG_GROUPS):
        gs = slice(g * SG_GROUP_DIM, (g + 1) * SG_GROUP_DIM)
        v_wide = jnp.concatenate([vn[c * SG_CHUNK:(c + 1) * SG_CHUNK, gs] for c in range(sg_chunks)], axis=1)
        per_group.append(_dot(w_sp[g], v_wide))
    mixed = jnp.concatenate(
        [jnp.concatenate([per_group[g][:, c * SG_GROUP_DIM:(c + 1) * SG_GROUP_DIM] for g in range(SG_GROUPS)], axis=1)
         + bias for c in range(sg_chunks)], axis=0)
    z = (u * mixed).astype(_BF)
    y_a = _dot(z, sg_wout_ref[...])

    sprev_l = {}
    for hd in range(GLA_HEADS):
        s_run = state_ref[hd]
        for c in range(n_chunks):
            sprev_l[c, hd] = s_run.astype(_BF)
            s_run = dcol_l[c, hd] * s_run + kv_l[c, hd]
        state_ref[hd] = s_run

    for c in range(n_chunks):
        r0 = c * GLA_CHUNK
        for hd in range(GLA_HEADS):
            ks = slice(hd * GLA_HEAD_K, (hd + 1) * GLA_HEAD_K)
            vs = slice(hd * GLA_HEAD_V, (hd + 1) * GLA_HEAD_V)
            lhs = jnp.concatenate([qdec_l[c][:, ks], scores_l[c, hd]], axis=1)
            rhs = jnp.concatenate([sprev_l[c, hd], vb[r0:r0 + GLA_CHUNK, vs]], axis=0)
            o_scr[r0:r0 + GLA_CHUNK, vs] = _dot(lhs, rhs)
    gn_g = gn_g_ref[...]
    o_heads = []
    for hd in range(GLA_HEADS):
        oh = o_scr[:, hd * GLA_HEAD_V:(hd + 1) * GLA_HEAD_V]
        o_heads.append(_rms(oh, gn_g))
    o_n = jnp.concatenate(o_heads, axis=1)
    o_g = (o_n * (og * _sigmoid(og))).astype(_BF)

    half = ts // 2
    y_b = [_dot(o_g[r0:r0 + half], gla_wout_ref[...]) for r0 in (0, half)]
    for j, r0 in enumerate((0, half)):
        merged = (_sigmoid(bg[r0:r0 + half, :D_MODEL]) * y_a[r0:r0 + half]
                  + _sigmoid(bg[r0:r0 + half, D_MODEL:]) * y_b[j])
        out_ref[r0:r0 + half, :] = x[r0:r0 + half] + _dot(merged.astype(_BF), wo_ref[...])


def _ffn_kernel(x_ref, p_ref, g_ffn_ref, w_up_ref, w_down_ref, g_ple_ref, w_pg_ref, w_pp_ref,
                g_fin_ref, out_ref):
    x = x_ref[...]
    hb = _rms(x, g_ffn_ref[...]).astype(_BF)
    ple = _dot(p_ref[...].astype(_BF), w_pp_ref[...])
    acc = x
    for f0 in range(0, D_FF, FFN_SLAB):
        up = jnp.maximum(_dot(hb, w_up_ref[:, f0:f0 + FFN_SLAB]), 0.0)
        acc = acc + _dot((up * up).astype(_BF), w_down_ref[f0:f0 + FFN_SLAB, :])
    half = x_ref.shape[0] // 2
    for r0 in (0, half):
        acc_h = acc[r0:r0 + half]
        h3 = _rms(acc_h, g_ple_ref[...]).astype(_BF)
        gate = _sigmoid(_dot(h3, w_pg_ref[...]))
        out_ref[r0:r0 + half, :] = _rms(acc_h + gate * ple[r0:r0 + half], g_fin_ref[...])


def _resident(shape):
    zeros = (0,) * len(shape)
    return pl.BlockSpec(shape, lambda *_: zeros, pipeline_mode=pl.Buffered(1))


def _slab_specs(arr, steps, step_of):
    rows = arr.shape[1]
    n = max(k for k in range(1, steps + 1) if rows % k == 0 and (rows // k) % BF16_SUBLANES == 0)
    in_spec = pl.BlockSpec((None, rows // n, arr.shape[2]), lambda *g: (0, jnp.minimum(step_of(*g), n - 1), 0))
    out_spec = pl.BlockSpec((rows // n, arr.shape[2]), lambda *g: (jnp.minimum(step_of(*g), n - 1), 0))
    return in_spec, out_spec


def _bf16_like(arr):
    return jax.ShapeDtypeStruct(arr.shape[1:], _BF)


def kernel(x, p, norm_mix_g, w_in, sg_ln_g, sg_ln_b, sg_w_s, sg_b_s, sg_w_out, gla_w_gate_up, gla_b_gate, gla_norm_g, gla_w_out, w_o, norm_ffn_g, ffn_w_up, ffn_w_down, ple_norm_g, ple_w_gate, ple_w_proj, final_norm_g):
    batch, seq, d = x.shape
    assert w_in.shape[0] == 1
    assert d == D_MODEL and seq % MIX_TILE == 0 and (batch * seq) % FFN_TILE == 0
    tokens = batch * seq
    i = 0

    mix_weights = (jnp.swapaxes(w_in, 1, 2), sg_w_out, gla_w_gate_up, gla_w_out, w_o)
    side_weights = (ffn_w_up, ffn_w_down, ple_w_gate, ple_w_proj)

    bias_sp = jnp.repeat(sg_b_s[i].T, SG_GROUP_DIM, axis=1)
    pos = jnp.arange(CUMSUM_ROWS)
    lmat = ((pos[:, None] // GLA_CHUNK == pos[None, :] // GLA_CHUNK) & (pos[None, :] <= pos[:, None])).astype(_BF)

    n_seq = seq // MIX_TILE
    n_tiles = batch * n_seq

    def tile_of(step):
        return jnp.maximum(step - MIX_PREP_STEPS, 0)

    x_spec = pl.BlockSpec((None, MIX_TILE, D_MODEL), lambda s: (tile_of(s) // n_seq, tile_of(s) % n_seq, 0))
    def next_tile(s):
        t = jnp.minimum(tile_of(s) + 1, n_tiles - 1)
        return (t // n_seq, t % n_seq, 0)

    xnext_spec = pl.BlockSpec((None, MIX_TILE, D_MODEL), next_tile)
    prep_specs = [_slab_specs(a, MIX_PREP_STEPS, lambda s: s) for a in mix_weights]
    side_specs = [_slab_specs(a, n_tiles, tile_of) for a in side_weights]
    pre = (norm_mix_g[i].reshape(1, D_MODEL), lmat)
    post = (sg_ln_g[i].reshape(1, SG_WIDTH), sg_ln_b[i].reshape(1, SG_WIDTH), sg_w_s[i], bias_sp,
            gla_b_gate[i].reshape(1, GLA_DK), gla_norm_g[i].reshape(1, GLA_HEAD_V))
    x, w_up_b, w_down_b, w_pg_b, w_pp_b = pl.pallas_call(
        functools.partial(_mix_kernel, n_seq=n_seq, prep_steps=MIX_PREP_STEPS),
        grid=(MIX_PREP_STEPS + n_tiles,),
        in_specs=[x_spec, xnext_spec] + [_resident(a.shape) for a in pre] + [sp[0] for sp in prep_specs]
        + [_resident(a.shape) for a in post] + [sp[0] for sp in side_specs],
        out_specs=[x_spec] + [sp[1] for sp in side_specs],
        out_shape=[jax.ShapeDtypeStruct((batch, seq, D_MODEL), _F32)] + [_bf16_like(a) for a in side_weights],
        scratch_shapes=[
            pltpu.VMEM((GLA_HEADS, GLA_HEAD_K, GLA_HEAD_V), _F32),
            pltpu.VMEM((MIX_TILE, GLA_DV), _F32),
        ] + [pltpu.VMEM(a.shape[1:], _BF) for a in mix_weights]
        + [pltpu.VMEM((2, MIX_TILE, D_MODEL), _BF)],
        compiler_params=pltpu.CompilerParams(
            dimension_semantics=("arbitrary",),
            vmem_limit_bytes=MIX_VMEM_LIMIT_BYTES,
        ),
        name="token_mixing",
    )(x, x, *pre, *mix_weights, *post, *side_weights)

    ffn_inputs = (
        x.reshape(tokens, D_MODEL),
        p[i].reshape(tokens, PLE_DIM),
        norm_ffn_g[i].reshape(1, D_MODEL),
        w_up_b,
        w_down_b,
        ple_norm_g[i].reshape(1, D_MODEL),
        w_pg_b,
        w_pp_b,
        final_norm_g.reshape(1, D_MODEL),
    )
    tok_spec = pl.BlockSpec((FFN_TILE, D_MODEL), lambda t: (t, 0))
    out = pl.pallas_call(
        _ffn_kernel,
        grid=(tokens // FFN_TILE,),
        in_specs=[tok_spec, pl.BlockSpec((FFN_TILE, PLE_DIM), lambda t: (t, 0))]
        + [_resident(a.shape) for a in ffn_inputs[2:]],
        out_specs=tok_spec,
        out_shape=jax.ShapeDtypeStruct((tokens, D_MODEL), _F32),
        compiler_params=pltpu.CompilerParams(
            dimension_semantics=("arbitrary",),
            vmem_limit_bytes=VMEM_LIMIT_BYTES,
        ),
        name="channel_mixing",
    )(*ffn_inputs)
    return out.reshape(batch, seq, D_MODEL)
```

```python
import functools

import jax
import jax.numpy as jnp
from jax import lax
from jax.experimental import pallas as pl
from jax.experimental.pallas import tpu as pltpu

D_MODEL = 1024
PLE_DIM = 256
SG_GROUPS = 4
SG_CHUNK = 128
SG_WIDTH = D_MODEL // 2
SG_GROUP_DIM = SG_WIDTH // SG_GROUPS
GLA_HEADS = 4
GLA_DK = D_MODEL // 2
GLA_DV = D_MODEL
GLA_HEAD_K = GLA_DK // GLA_HEADS
GLA_HEAD_V = GLA_DV // GLA_HEADS
GLA_GATE_RANK = 16
GLA_GATE_TEMP = 16.0
GLA_CHUNK = 64
D_FF = 4 * D_MODEL
EPS = 1e-6

_C_U = 0
_C_Q = _C_U + 2 * SG_WIDTH
_C_LR = _C_Q + 2 * GLA_DK + GLA_DV
_C_OG = _C_LR + GLA_GATE_RANK
_C_BA = _C_OG + GLA_DV
_C_END = _C_BA + 2 * D_MODEL

MIX_TILE = 512
CUMSUM_ROWS = 256
MIX_PREP_STEPS = 11
BF16_SUBLANES = 16
FFN_TILE = 1024
FFN_SLAB = 1024
VMEM_LIMIT_BYTES = 56 * 1024 * 1024
MIX_VMEM_LIMIT_BYTES = 62 * 1024 * 1024

_BF = jnp.bfloat16
_F32 = jnp.float32

_dot = functools.partial(jnp.dot, preferred_element_type=_F32)


def _rms(x, g):
    ms = jnp.mean(x * x, axis=-1, keepdims=True)
    return x * lax.rsqrt(ms + EPS) * g


def _gelu_tanh(x):
    c = 0.7978845608028654
    return x * (0.5 * (1.0 + jnp.tanh(c * (x + 0.044715 * (x * x * x)))))


def _sigmoid(x):
    return 0.5 * jnp.tanh(0.5 * x) + 0.5


def _log_sigmoid(x):
    return jnp.minimum(x, 0.0) - jnp.log1p(jnp.exp(-jnp.abs(x)))


def _cast_blocks(src_refs, dst_refs):
    for src, dst in zip(src_refs, dst_refs):
        dst[...] = src[...].astype(_BF)


def _mix_kernel(x_ref, xnext_ref, g_ref, lmat_ref, w_in_t_f, sg_wout_f, wgu_f, gla_wout_f, wo_f,
                ln_g_ref, ln_b_ref, ws_ref, bs_ref, bgate_ref, gn_g_ref,
                w_up_f, w_down_f, w_pg_f, w_pp_f,
                out_ref, w_up_o, w_down_o, w_pg_o, w_pp_o,
                state_ref, o_scr, w_in_t_ref, sg_wout_ref, wgu_ref, gla_wout_ref, wo_ref, hb_scr, hb_next_scr,
                *, n_seq, prep_steps):
    step = pl.program_id(0)

    @pl.when(step < prep_steps)
    def _():
        for src, dst in ((w_in_t_f, w_in_t_ref), (sg_wout_f, sg_wout_ref), (wgu_f, wgu_ref),
                         (gla_wout_f, gla_wout_ref), (wo_f, wo_ref)):
            rows = src.shape[0]
            n_slabs = dst.shape[0] // rows
            if n_slabs == 1:
                dst[...] = src[...].astype(_BF)
            else:
                r0 = pl.multiple_of(jnp.minimum(step, n_slabs - 1) * rows, rows)
                dst[pl.ds(r0, rows), :] = src[...].astype(_BF)

    @pl.when(step == prep_steps - 1)
    def _():
        hb_next_scr[...] = _rms(x_ref[...], g_ref[...]).astype(_BF)

    @pl.when(step >= prep_steps)
    def _():
        _mix_tile((step - prep_steps) % n_seq == 0, hb_scr, hb_next_scr, xnext_ref,
                  x_ref, g_ref, lmat_ref, w_in_t_ref, ln_g_ref, ln_b_ref, ws_ref, bs_ref,
                  sg_wout_ref, wgu_ref, bgate_ref, gn_g_ref, gla_wout_ref, wo_ref,
                  w_up_f, w_down_f, w_pg_f, w_pp_f,
                  out_ref, w_up_o, w_down_o, w_pg_o, w_pp_o, state_ref, o_scr)


def _mix_tile(first_of_sequence, hb_scr, hb_next_scr, xnext_ref, x_ref, g_ref, lmat_ref, w_in_t_ref, ln_g_ref, ln_b_ref, ws_ref, bs_ref,
              sg_wout_ref, wgu_ref, bgate_ref, gn_g_ref, gla_wout_ref, wo_ref,
              w_up_f, w_down_f, w_pg_f, w_pp_f,
              out_ref, w_up_o, w_down_o, w_pg_o, w_pp_o, state_ref, o_scr):
    ts = x_ref.shape[0]

    def proj(lo, hi):
        return lax.dot_general(hb, w_in_t_ref[lo:hi, :], (((1,), (1,)), ((), ())), preferred_element_type=_F32)

    @pl.when(first_of_sequence)
    def _():
        state_ref[...] = jnp.zeros_like(state_ref)

    x = x_ref[...]
    hb_scr[...] = hb_next_scr[...]
    hb = hb_scr[...]
    hb_next_scr[...] = _rms(xnext_ref[...], g_ref[...]).astype(_BF)

    glr = proj(_C_LR, _C_OG)
    uv = proj(_C_U, _C_Q)
    zg = _dot(glr.astype(_BF), wgu_ref[...]) + bgate_ref[...]
    qkv = proj(_C_Q, _C_LR)

    log_a = _log_sigmoid(zg) * (1.0 / GLA_GATE_TEMP)
    la_hi = log_a.astype(_BF)
    la_lo = (log_a - la_hi.astype(_F32)).astype(_BF)
    lmat = lmat_ref[...]
    b = jnp.concatenate(
        [_dot(lmat, la_hi[r0:r0 + CUMSUM_ROWS]) + _dot(lmat, la_lo[r0:r0 + CUMSUM_ROWS])
         for r0 in range(0, ts, CUMSUM_ROWS)], axis=0)

    u = _gelu_tanh(uv[:, :SG_WIDTH])
    v = _gelu_tanh(uv[:, SG_WIDTH:])
    mu = jnp.mean(v, axis=-1, keepdims=True)
    vc = v - mu
    var = jnp.mean(vc * vc, axis=-1, keepdims=True)
    vn = (vc * lax.rsqrt(var + EPS) * ln_g_ref[...] + ln_b_ref[...]).astype(_BF)

    og = proj(_C_OG, _C_BA)

    _cast_blocks((w_up_f, w_down_f, w_pg_f, w_pp_f), (w_up_o, w_down_o, w_pg_o, w_pp_o))

    q = qkv[:, :GLA_DK] * (GLA_HEAD_K ** -0.5)
    k = qkv[:, GLA_DK:2 * GLA_DK]
    vb = qkv[:, 2 * GLA_DK:].astype(_BF)
    r64 = lax.broadcasted_iota(jnp.int32, (GLA_CHUNK, GLA_CHUNK), 0)
    c64 = lax.broadcasted_iota(jnp.int32, (GLA_CHUNK, GLA_CHUNK), 1)
    causal = r64 >= c64
    rk = lax.broadcasted_iota(jnp.int32, (GLA_HEAD_K, GLA_HEAD_K), 0)
    ck = lax.broadcasted_iota(jnp.int32, (GLA_HEAD_K, GLA_HEAD_K), 1)
    eye_k = rk == ck
    n_chunks = ts // GLA_CHUNK
    scores_l, kv_l, qdec_l, dcol_l = {}, {}, {}, {}
    for c in range(n_chunks):
        r0 = c * GLA_CHUNK
        bc = b[r0:r0 + GLA_CHUNK]
        b_mid = bc[GLA_CHUNK // 2 - 1:GLA_CHUNK // 2]
        b_last = bc[GLA_CHUNK - 1:GLA_CHUNK]
        qc = q[r0:r0 + GLA_CHUNK]
        kc = k[r0:r0 + GLA_CHUNK]
        q_in = (qc * jnp.exp(bc - b_mid)).astype(_BF)
        k_in = (kc * jnp.exp(b_mid - bc)).astype(_BF)
        k_dec = (kc * jnp.exp(b_last - bc)).astype(_BF)
        qdec_l[c] = (qc * jnp.exp(bc)).astype(_BF)
        dec = jnp.exp(b_last)
        for hd in range(GLA_HEADS):
            ks = slice(hd * GLA_HEAD_K, (hd + 1) * GLA_HEAD_K)
            vs = slice(hd * GLA_HEAD_V, (hd + 1) * GLA_HEAD_V)
            sc = lax.dot_general(q_in[:, ks], k_in[:, ks], (((1,), (1,)), ((), ())),
                                 preferred_element_type=_F32)
            scores_l[c, hd] = jnp.where(causal, sc, 0.0).astype(_BF)
            kv_l[c, hd] = lax.dot_general(k_dec[:, ks], vb[r0:r0 + GLA_CHUNK, vs], (((0,), (0,)), ((), ())),
                                          preferred_element_type=_F32)
            dcol_l[c, hd] = jnp.sum(
                jnp.where(eye_k, jnp.broadcast_to(dec[:, ks], (GLA_HEAD_K, GLA_HEAD_K)), 0.0),
                axis=1, keepdims=True)

    bg = proj(_C_BA, _C_END)

    row = lax.broadcasted_iota(jnp.int32, (SG_CHUNK, SG_CHUNK), 0)
    col = lax.broadcasted_iota(jnp.int32, (SG_CHUNK, SG_CHUNK), 1)
    tril = row >= col
    w_sp = [jnp.where(tril, ws_ref[g], 0.0).astype(_BF) for g in range(SG_GROUPS)]
    bias = bs_ref[...]
    sg_chunks = ts // SG_CHUNK
    per_group = []
    for g in range(SG_GROUPS):
        gs = slice(g * SG_GROUP_DIM, (g + 1) * SG_GROUP_DIM)
        v_wide = jnp.concatenate([vn[c * SG_CHUNK:(c + 1) * SG_CHUNK, gs] for c in range(sg_chunks)], axis=1)
        per_group.append(_dot(w_sp[g], v_wide))
    mixed = jnp.concatenate(
        [jnp.concatenate([per_group[g][:, c * SG_GROUP_DIM:(c + 1) * SG_GROUP_DIM] for g in range(SG_GROUPS)], axis=1)
         + bias for c in range(sg_chunks)], axis=0)
    z = (u * mixed).astype(_BF)
    y_a = _dot(z, sg_wout_ref[...])

    sprev_l = {}
    for hd in range(GLA_HEADS):
        s_run = state_ref[hd]
        for c in range(n_chunks):
            sprev_l[c, hd] = s_run.astype(_BF)
            s_run = dcol_l[c, hd] * s_run + kv_l[c, hd]
        state_ref[hd] = s_run

    for c in range(n_chunks):
        r0 = c * GLA_CHUNK
        for hd in range(GLA_HEADS):
            ks = slice(hd * GLA_HEAD_K, (hd + 1) * GLA_HEAD_K)
            vs = slice(hd * GLA_HEAD_V, (hd + 1) * GLA_HEAD_V)
            lhs = jnp.concatenate([qdec_l[c][:, ks], scores_l[c, hd]], axis=1)
            rhs = jnp.concatenate([sprev_l[c, hd], vb[r0:r0 + GLA_CHUNK, vs]], axis=0)
            o_scr[r0:r0 + GLA_CHUNK, vs] = _dot(lhs, rhs)
    gn_g = gn_g_ref[...]
    o_heads = []
    for hd in range(GLA_HEADS):
        oh = o_scr[:, hd * GLA_HEAD_V:(hd + 1) * GLA_HEAD_V]
        o_heads.append(_rms(oh, gn_g))
    o_n = jnp.concatenate(o_heads, axis=1)
    o_g = (o_n * (og * _sigmoid(og))).astype(_BF)

    half = ts // 2
    y_b = [_dot(o_g[r0:r0 + half], gla_wout_ref[...]) for r0 in (0, half)]
    for j, r0 in enumerate((0, half)):
        merged = (_sigmoid(bg[r0:r0 + half, :D_MODEL]) * y_a[r0:r0 + half]
                  + _sigmoid(bg[r0:r0 + half, D_MODEL:]) * y_b[j])
        out_ref[r0:r0 + half, :] = x[r0:r0 + half] + _dot(merged.astype(_BF), wo_ref[...])


def _ffn_kernel(x_ref, p_ref, g_ffn_ref, w_up_ref, w_down_ref, g_ple_ref, w_pg_ref, w_pp_ref,
                g_fin_ref, out_ref):
    x = x_ref[...]
    hb = _rms(x, g_ffn_ref[...]).astype(_BF)
    ple = _dot(p_ref[...].astype(_BF), w_pp_ref[...])
    acc = x
    for f0 in range(0, D_FF, FFN_SLAB):
        up = jnp.maximum(_dot(hb, w_up_ref[:, f0:f0 + FFN_SLAB]), 0.0)
        acc = acc + _dot((up * up).astype(_BF), w_down_ref[f0:f0 + FFN_SLAB, :])
    half = x_ref.shape[0] // 2
    for r0 in (0, half):
        acc_h = acc[r0:r0 + half]
        h3 = _rms(acc_h, g_ple_ref[...]).astype(_BF)
        gate = _sigmoid(_dot(h3, w_pg_ref[...]))
        out_ref[r0:r0 + half, :] = _rms(acc_h + gate * ple[r0:r0 + half], g_fin_ref[...])


def _resident(shape):
    zeros = (0,) * len(shape)
    return pl.BlockSpec(shape, lambda *_: zeros, pipeline_mode=pl.Buffered(1))


def _slab_specs(arr, steps, step_of):
    rows = arr.shape[1]
    n = max(k for k in range(1, steps + 1) if rows % k == 0 and (rows // k) % BF16_SUBLANES == 0)
    in_spec = pl.BlockSpec((None, rows // n, arr.shape[2]), lambda *g: (0, jnp.minimum(step_of(*g), n - 1), 0))
    out_spec = pl.BlockSpec((rows // n, arr.shape[2]), lambda *g: (jnp.minimum(step_of(*g), n - 1), 0))
    return in_spec, out_spec


def _bf16_like(arr):
    return jax.ShapeDtypeStruct(arr.shape[1:], _BF)


def kernel(x, p, norm_mix_g, w_in, sg_ln_g, sg_ln_b, sg_w_s, sg_b_s, sg_w_out, gla_w_gate_up, gla_b_gate, gla_norm_g, gla_w_out, w_o, norm_ffn_g, ffn_w_up, ffn_w_down, ple_norm_g, ple_w_gate, ple_w_proj, final_norm_g):
    batch, seq, d = x.shape
    assert w_in.shape[0] == 1
    assert d == D_MODEL and seq % MIX_TILE == 0 and (batch * seq) % FFN_TILE == 0
    tokens = batch * seq
    i = 0

    mix_weights = (jnp.swapaxes(w_in, 1, 2), sg_w_out, gla_w_gate_up, gla_w_out, w_o)
    side_weights = (ffn_w_up, ffn_w_down, ple_w_gate, ple_w_proj)

    bias_sp = jnp.repeat(sg_b_s[i].T, SG_GROUP_DIM, axis=1)
    pos = jnp.arange(CUMSUM_ROWS)
    lmat = ((pos[:, None] // GLA_CHUNK == pos[None, :] // GLA_CHUNK) & (pos[None, :] <= pos[:, None])).astype(_BF)

    n_seq = seq // MIX_TILE
    n_tiles = batch * n_seq

    def tile_of(step):
        return jnp.maximum(step - MIX_PREP_STEPS, 0)

    x_spec = pl.BlockSpec((None, MIX_TILE, D_MODEL), lambda s: (tile_of(s) // n_seq, tile_of(s) % n_seq, 0))
    def next_tile(s):
        t = jnp.minimum(tile_of(s) + 1, n_tiles - 1)
        return (t // n_seq, t % n_seq, 0)

    xnext_spec = pl.BlockSpec((None, MIX_TILE, D_MODEL), next_tile)
    prep_specs = [_slab_specs(a, MIX_PREP_STEPS, lambda s: s) for a in mix_weights]
    side_specs = [_slab_specs(a, n_tiles, tile_of) for a in side_weights]
    pre = (norm_mix_g[i].reshape(1, D_MODEL), lmat)
    post = (sg_ln_g[i].reshape(1, SG_WIDTH), sg_ln_b[i].reshape(1, SG_WIDTH), sg_w_s[i], bias_sp,
            gla_b_gate[i].reshape(1, GLA_DK), gla_norm_g[i].reshape(1, GLA_HEAD_V))
    x, w_up_b, w_down_b, w_pg_b, w_pp_b = pl.pallas_call(
        functools.partial(_mix_kernel, n_seq=n_seq, prep_steps=MIX_PREP_STEPS),
        grid=(MIX_PREP_STEPS + n_tiles,),
        in_specs=[x_spec, xnext_spec] + [_resident(a.shape) for a in pre] + [sp[0] for sp in prep_specs]
        + [_resident(a.shape) for a in post] + [sp[0] for sp in side_specs],
        out_specs=[x_spec] + [sp[1] for sp in side_specs],
        out_shape=[jax.ShapeDtypeStruct((batch, seq, D_MODEL), _F32)] + [_bf16_like(a) for a in side_weights],
        scratch_shapes=[
            pltpu.VMEM((GLA_HEADS, GLA_HEAD_K, GLA_HEAD_V), _F32),
            pltpu.VMEM((MIX_TILE, GLA_DV), _F32),
        ] + [pltpu.VMEM(a.shape[1:], _BF) for a in mix_weights]
        + [pltpu.VMEM((MIX_TILE, D_MODEL), _BF)] * 2,
        compiler_params=pltpu.CompilerParams(
            dimension_semantics=("arbitrary",),
            vmem_limit_bytes=MIX_VMEM_LIMIT_BYTES,
        ),
        name="token_mixing",
    )(x, x, *pre, *mix_weights, *post, *side_weights)

    ffn_inputs = (
        x.reshape(tokens, D_MODEL),
        p[i].reshape(tokens, PLE_DIM),
        norm_ffn_g[i].reshape(1, D_MODEL),
        w_up_b,
        w_down_b,
        ple_norm_g[i].reshape(1, D_MODEL),
        w_pg_b,
        w_pp_b,
        final_norm_g.reshape(1, D_MODEL),
    )
    tok_spec = pl.BlockSpec((FFN_TILE, D_MODEL), lambda t: (t, 0))
    out = pl.pallas_call(
        _ffn_kernel,
        grid=(tokens // FFN_TILE,),
        in_specs=[tok_spec, pl.BlockSpec((FFN_TILE, PLE_DIM), lambda t: (t, 0))]
        + [_resident(a.shape) for a in ffn_inputs[2:]],
        out_specs=tok_spec,
        out_shape=jax.ShapeDtypeStruct((tokens, D_MODEL), _F32),
        compiler_params=pltpu.CompilerParams(
            dimension_semantics=("arbitrary",),
            vmem_limit_bytes=VMEM_LIMIT_BYTES,
        ),
        name="channel_mixing",
    )(*ffn_inputs)
    return out.reshape(batch, seq, D_MODEL)
```

```python
import functools

import jax
import jax.numpy as jnp
from jax import lax
from jax.experimental import pallas as pl
from jax.experimental.pallas import tpu as pltpu

D_MODEL = 1024
PLE_DIM = 256
SG_GROUPS = 4
SG_CHUNK = 128
SG_WIDTH = D_MODEL // 2
SG_GROUP_DIM = SG_WIDTH // SG_GROUPS
GLA_HEADS = 4
GLA_DK = D_MODEL // 2
GLA_DV = D_MODEL
GLA_HEAD_K = GLA_DK // GLA_HEADS
GLA_HEAD_V = GLA_DV // GLA_HEADS
GLA_GATE_RANK = 16
GLA_GATE_TEMP = 16.0
GLA_CHUNK = 64
D_FF = 4 * D_MODEL
EPS = 1e-6

_C_U = 0
_C_Q = _C_U + 2 * SG_WIDTH
_C_LR = _C_Q + 2 * GLA_DK + GLA_DV
_C_OG = _C_LR + GLA_GATE_RANK
_C_BA = _C_OG + GLA_DV
_C_END = _C_BA + 2 * D_MODEL

MIX_TILE = 512
CUMSUM_ROWS = 256
MIX_PREP_STEPS = 11
BF16_SUBLANES = 16
FFN_TILE = 1024
FFN_SLAB = 1024
VMEM_LIMIT_BYTES = 56 * 1024 * 1024

_BF = jnp.bfloat16
_F32 = jnp.float32

_dot = functools.partial(jnp.dot, preferred_element_type=_F32)


def _rms(x, g):
    ms = jnp.mean(x * x, axis=-1, keepdims=True)
    return x * lax.rsqrt(ms + EPS) * g


def _gelu_tanh(x):
    c = 0.7978845608028654
    return x * (0.5 * (1.0 + jnp.tanh(c * (x + 0.044715 * (x * x * x)))))


def _sigmoid(x):
    return 0.5 * jnp.tanh(0.5 * x) + 0.5


def _log_sigmoid(x):
    return jnp.minimum(x, 0.0) - jnp.log1p(jnp.exp(-jnp.abs(x)))


def _cast_blocks(src_refs, dst_refs):
    for src, dst in zip(src_refs, dst_refs):
        dst[...] = src[...].astype(_BF)


def _mix_kernel(x_ref, g_ref, lmat_ref, w_in_t_f, sg_wout_f, wgu_f, gla_wout_f, wo_f,
                ln_g_ref, ln_b_ref, ws_ref, bs_ref, bgate_ref, gn_g_ref,
                w_up_f, w_down_f, w_pg_f, w_pp_f,
                out_ref, w_up_o, w_down_o, w_pg_o, w_pp_o,
                state_ref, o_scr, w_in_t_ref, sg_wout_ref, wgu_ref, gla_wout_ref, wo_ref, *, n_seq, prep_steps):
    step = pl.program_id(0)

    @pl.when(step < prep_steps)
    def _():
        for src, dst in ((w_in_t_f, w_in_t_ref), (sg_wout_f, sg_wout_ref), (wgu_f, wgu_ref),
                         (gla_wout_f, gla_wout_ref), (wo_f, wo_ref)):
            rows = src.shape[0]
            n_slabs = dst.shape[0] // rows
            if n_slabs == 1:
                dst[...] = src[...].astype(_BF)
            else:
                r0 = pl.multiple_of(jnp.minimum(step, n_slabs - 1) * rows, rows)
                dst[pl.ds(r0, rows), :] = src[...].astype(_BF)

    @pl.when(step >= prep_steps)
    def _():
        _mix_tile((step - prep_steps) % n_seq == 0,
                  x_ref, g_ref, lmat_ref, w_in_t_ref, ln_g_ref, ln_b_ref, ws_ref, bs_ref,
                  sg_wout_ref, wgu_ref, bgate_ref, gn_g_ref, gla_wout_ref, wo_ref,
                  w_up_f, w_down_f, w_pg_f, w_pp_f,
                  out_ref, w_up_o, w_down_o, w_pg_o, w_pp_o, state_ref, o_scr)


def _mix_tile(first_of_sequence, x_ref, g_ref, lmat_ref, w_in_t_ref, ln_g_ref, ln_b_ref, ws_ref, bs_ref,
              sg_wout_ref, wgu_ref, bgate_ref, gn_g_ref, gla_wout_ref, wo_ref,
              w_up_f, w_down_f, w_pg_f, w_pp_f,
              out_ref, w_up_o, w_down_o, w_pg_o, w_pp_o, state_ref, o_scr):
    ts = x_ref.shape[0]

    def proj(lo, hi):
        return lax.dot_general(hb, w_in_t_ref[lo:hi, :], (((1,), (1,)), ((), ())), preferred_element_type=_F32)

    @pl.when(first_of_sequence)
    def _():
        state_ref[...] = jnp.zeros_like(state_ref)

    x = x_ref[...]
    hb = _rms(x, g_ref[...]).astype(_BF)

    glr = proj(_C_LR, _C_OG)
    uv = proj(_C_U, _C_Q)
    zg = _dot(glr.astype(_BF), wgu_ref[...]) + bgate_ref[...]
    qkv = proj(_C_Q, _C_LR)

    log_a = _log_sigmoid(zg) * (1.0 / GLA_GATE_TEMP)
    la_hi = log_a.astype(_BF)
    la_lo = (log_a - la_hi.astype(_F32)).astype(_BF)
    lmat = lmat_ref[...]
    b = jnp.concatenate(
        [_dot(lmat, la_hi[r0:r0 + CUMSUM_ROWS]) + _dot(lmat, la_lo[r0:r0 + CUMSUM_ROWS])
         for r0 in range(0, ts, CUMSUM_ROWS)], axis=0)

    u = _gelu_tanh(uv[:, :SG_WIDTH])
    v = _gelu_tanh(uv[:, SG_WIDTH:])
    mu = jnp.mean(v, axis=-1, keepdims=True)
    vc = v - mu
    var = jnp.mean(vc * vc, axis=-1, keepdims=True)
    vn = (vc * lax.rsqrt(var + EPS) * ln_g_ref[...] + ln_b_ref[...]).astype(_BF)

    og = proj(_C_OG, _C_BA)
    bg = proj(_C_BA, _C_END)

    _cast_blocks((w_up_f, w_down_f, w_pg_f, w_pp_f), (w_up_o, w_down_o, w_pg_o, w_pp_o))

    row = lax.broadcasted_iota(jnp.int32, (SG_CHUNK, SG_CHUNK), 0)
    col = lax.broadcasted_iota(jnp.int32, (SG_CHUNK, SG_CHUNK), 1)
    tril = row >= col
    w_sp = [jnp.where(tril, ws_ref[g], 0.0).astype(_BF) for g in range(SG_GROUPS)]
    bias = bs_ref[...]
    sg_chunks = ts // SG_CHUNK
    per_group = []
    for g in range(SG_GROUPS):
        gs = slice(g * SG_GROUP_DIM, (g + 1) * SG_GROUP_DIM)
        v_wide = jnp.concatenate([vn[c * SG_CHUNK:(c + 1) * SG_CHUNK, gs] for c in range(sg_chunks)], axis=1)
        per_group.append(_dot(w_sp[g], v_wide))
    mixed = jnp.concatenate(
        [jnp.concatenate([per_group[g][:, c * SG_GROUP_DIM:(c + 1) * SG_GROUP_DIM] for g in range(SG_GROUPS)], axis=1)
         + bias for c in range(sg_chunks)], axis=0)
    z = (u * mixed).astype(_BF)
    y_a = _dot(z, sg_wout_ref[...])

    q = qkv[:, :GLA_DK] * (GLA_HEAD_K ** -0.5)
    k = qkv[:, GLA_DK:2 * GLA_DK]
    vb = qkv[:, 2 * GLA_DK:].astype(_BF)
    r64 = lax.broadcasted_iota(jnp.int32, (GLA_CHUNK, GLA_CHUNK), 0)
    c64 = lax.broadcasted_iota(jnp.int32, (GLA_CHUNK, GLA_CHUNK), 1)
    causal = r64 >= c64
    rk = lax.broadcasted_iota(jnp.int32, (GLA_HEAD_K, GLA_HEAD_K), 0)
    ck = lax.broadcasted_iota(jnp.int32, (GLA_HEAD_K, GLA_HEAD_K), 1)
    eye_k = rk == ck
    n_chunks = ts // GLA_CHUNK
    scores_l, kv_l, qdec_l, dcol_l = {}, {}, {}, {}
    for c in range(n_chunks):
        r0 = c * GLA_CHUNK
        bc = b[r0:r0 + GLA_CHUNK]
        b_mid = bc[GLA_CHUNK // 2 - 1:GLA_CHUNK // 2]
        b_last = bc[GLA_CHUNK - 1:GLA_CHUNK]
        qc = q[r0:r0 + GLA_CHUNK]
        kc = k[r0:r0 + GLA_CHUNK]
        q_in = (qc * jnp.exp(bc - b_mid)).astype(_BF)
        k_in = (kc * jnp.exp(b_mid - bc)).astype(_BF)
        k_dec = (kc * jnp.exp(b_last - bc)).astype(_BF)
        qdec_l[c] = (qc * jnp.exp(bc)).astype(_BF)
        dec = jnp.exp(b_last)
        for hd in range(GLA_HEADS):
            ks = slice(hd * GLA_HEAD_K, (hd + 1) * GLA_HEAD_K)
            vs = slice(hd * GLA_HEAD_V, (hd + 1) * GLA_HEAD_V)
            sc = lax.dot_general(q_in[:, ks], k_in[:, ks], (((1,), (1,)), ((), ())),
                                 preferred_element_type=_F32)
            scores_l[c, hd] = jnp.where(causal, sc, 0.0).astype(_BF)
            kv_l[c, hd] = lax.dot_general(k_dec[:, ks], vb[r0:r0 + GLA_CHUNK, vs], (((0,), (0,)), ((), ())),
                                          preferred_element_type=_F32)
            dcol_l[c, hd] = jnp.sum(
                jnp.where(eye_k, jnp.broadcast_to(dec[:, ks], (GLA_HEAD_K, GLA_HEAD_K)), 0.0),
                axis=1, keepdims=True)

    sprev_l = {}
    for hd in range(GLA_HEADS):
        s_run = state_ref[hd]
        for c in range(n_chunks):
            sprev_l[c, hd] = s_run.astype(_BF)
            s_run = dcol_l[c, hd] * s_run + kv_l[c, hd]
        state_ref[hd] = s_run

    for hd in range(GLA_HEADS):
        for c in range(n_chunks):
            r0 = c * GLA_CHUNK
            ks = slice(hd * GLA_HEAD_K, (hd + 1) * GLA_HEAD_K)
            vs = slice(hd * GLA_HEAD_V, (hd + 1) * GLA_HEAD_V)
            lhs = jnp.concatenate([qdec_l[c][:, ks], scores_l[c, hd]], axis=1)
            rhs = jnp.concatenate([sprev_l[c, hd], vb[r0:r0 + GLA_CHUNK, vs]], axis=0)
            o_scr[r0:r0 + GLA_CHUNK, vs] = _dot(lhs, rhs)
    gn_g = gn_g_ref[...]
    o_heads = []
    for hd in range(GLA_HEADS):
        oh = o_scr[:, hd * GLA_HEAD_V:(hd + 1) * GLA_HEAD_V]
        o_heads.append(_rms(oh, gn_g))
    o_n = jnp.concatenate(o_heads, axis=1)
    o_g = (o_n * (og * _sigmoid(og))).astype(_BF)

    half = ts // 2
    y_b = [_dot(o_g[r0:r0 + half], gla_wout_ref[...]) for r0 in (0, half)]
    for j, r0 in enumerate((0, half)):
        merged = (_sigmoid(bg[r0:r0 + half, :D_MODEL]) * y_a[r0:r0 + half]
                  + _sigmoid(bg[r0:r0 + half, D_MODEL:]) * y_b[j])
        out_ref[r0:r0 + half, :] = x[r0:r0 + half] + _dot(merged.astype(_BF), wo_ref[...])


def _ffn_kernel(x_ref, p_ref, g_ffn_ref, w_up_ref, w_down_ref, g_ple_ref, w_pg_ref, w_pp_ref,
                g_fin_ref, out_ref):
    x = x_ref[...]
    hb = _rms(x, g_ffn_ref[...]).astype(_BF)
    ple = _dot(p_ref[...].astype(_BF), w_pp_ref[...])
    acc = x
    for f0 in range(0, D_FF, FFN_SLAB):
        up = jnp.maximum(_dot(hb, w_up_ref[:, f0:f0 + FFN_SLAB]), 0.0)
        acc = acc + _dot((up * up).astype(_BF), w_down_ref[f0:f0 + FFN_SLAB, :])
    half = x_ref.shape[0] // 2
    for r0 in (0, half):
        acc_h = acc[r0:r0 + half]
        h3 = _rms(acc_h, g_ple_ref[...]).astype(_BF)
        gate = _sigmoid(_dot(h3, w_pg_ref[...]))
        out_ref[r0:r0 + half, :] = _rms(acc_h + gate * ple[r0:r0 + half], g_fin_ref[...])


def _resident(shape):
    zeros = (0,) * len(shape)
    return pl.BlockSpec(shape, lambda *_: zeros, pipeline_mode=pl.Buffered(1))


def _slab_specs(arr, steps, step_of):
    rows = arr.shape[1]
    n = max(k for k in range(1, steps + 1) if rows % k == 0 and (rows // k) % BF16_SUBLANES == 0)
    in_spec = pl.BlockSpec((None, rows // n, arr.shape[2]), lambda *g: (0, jnp.minimum(step_of(*g), n - 1), 0))
    out_spec = pl.BlockSpec((rows // n, arr.shape[2]), lambda *g: (jnp.minimum(step_of(*g), n - 1), 0))
    return in_spec, out_spec


def _bf16_like(arr):
    return jax.ShapeDtypeStruct(arr.shape[1:], _BF)


def kernel(x, p, norm_mix_g, w_in, sg_ln_g, sg_ln_b, sg_w_s, sg_b_s, sg_w_out, gla_w_gate_up, gla_b_gate, gla_norm_g, gla_w_out, w_o, norm_ffn_g, ffn_w_up, ffn_w_down, ple_norm_g, ple_w_gate, ple_w_proj, final_norm_g):
    batch, seq, d = x.shape
    assert w_in.shape[0] == 1
    assert d == D_MODEL and seq % MIX_TILE == 0 and (batch * seq) % FFN_TILE == 0
    tokens = batch * seq
    i = 0

    mix_weights = (jnp.swapaxes(w_in, 1, 2), sg_w_out, gla_w_gate_up, gla_w_out, w_o)
    side_weights = (ffn_w_up, ffn_w_down, ple_w_gate, ple_w_proj)

    bias_sp = jnp.repeat(sg_b_s[i].T, SG_GROUP_DIM, axis=1)
    pos = jnp.arange(CUMSUM_ROWS)
    lmat = ((pos[:, None] // GLA_CHUNK == pos[None, :] // GLA_CHUNK) & (pos[None, :] <= pos[:, None])).astype(_BF)

    n_seq = seq // MIX_TILE
    n_tiles = batch * n_seq

    def tile_of(step):
        return jnp.maximum(step - MIX_PREP_STEPS, 0)

    x_spec = pl.BlockSpec((None, MIX_TILE, D_MODEL), lambda s: (tile_of(s) // n_seq, tile_of(s) % n_seq, 0))
    prep_specs = [_slab_specs(a, MIX_PREP_STEPS, lambda s: s) for a in mix_weights]
    side_specs = [_slab_specs(a, n_tiles, tile_of) for a in side_weights]
    pre = (norm_mix_g[i].reshape(1, D_MODEL), lmat)
    post = (sg_ln_g[i].reshape(1, SG_WIDTH), sg_ln_b[i].reshape(1, SG_WIDTH), sg_w_s[i], bias_sp,
            gla_b_gate[i].reshape(1, GLA_DK), gla_norm_g[i].reshape(1, GLA_HEAD_V))
    x, w_up_b, w_down_b, w_pg_b, w_pp_b = pl.pallas_call(
        functools.partial(_mix_kernel, n_seq=n_seq, prep_steps=MIX_PREP_STEPS),
        grid=(MIX_PREP_STEPS + n_tiles,),
        in_specs=[x_spec] + [_resident(a.shape) for a in pre] + [sp[0] for sp in prep_specs]
        + [_resident(a.shape) for a in post] + [sp[0] for sp in side_specs],
        out_specs=[x_spec] + [sp[1] for sp in side_specs],
        out_shape=[jax.ShapeDtypeStruct((batch, seq, D_MODEL), _F32)] + [_bf16_like(a) for a in side_weights],
        scratch_shapes=[
            pltpu.VMEM((GLA_HEADS, GLA_HEAD_K, GLA_HEAD_V), _F32),
            pltpu.VMEM((MIX_TILE, GLA_DV), _F32),
        ] + [pltpu.VMEM(a.shape[1:], _BF) for a in mix_weights],
        compiler_params=pltpu.CompilerParams(
            dimension_semantics=("arbitrary",),
            vmem_limit_bytes=VMEM_LIMIT_BYTES,
        ),
        name="token_mixing",
    )(x, *pre, *mix_weights, *post, *side_weights)

    ffn_inputs = (
        x.reshape(tokens, D_MODEL),
        p[i].reshape(tokens, PLE_DIM),
        norm_ffn_g[i].reshape(1, D_MODEL),
        w_up_b,
        w_down_b,
        ple_norm_g[i].reshape(1, D_MODEL),
        w_pg_b,
        w_pp_b,
        final_norm_g.reshape(1, D_MODEL),
    )
    tok_spec = pl.BlockSpec((FFN_TILE, D_MODEL), lambda t: (t, 0))
    out = pl.pallas_call(
        _ffn_kernel,
        grid=(tokens // FFN_TILE,),
        in_specs=[tok_spec, pl.BlockSpec((FFN_TILE, PLE_DIM), lambda t: (t, 0))]
        + [_resident(a.shape) for a in ffn_inputs[2:]],
        out_specs=tok_spec,
        out_shape=jax.ShapeDtypeStruct((tokens, D_MODEL), _F32),
        compiler_params=pltpu.CompilerParams(
            dimension_semantics=("arbitrary",),
            vmem_limit_bytes=VMEM_LIMIT_BYTES,
        ),
        name="channel_mixing",
    )(*ffn_inputs)
    return out.reshape(batch, seq, D_MODEL)
```

```python
import functools

import jax
import jax.numpy as jnp
from jax import lax
from jax.experimental import pallas as pl
from jax.experimental.pallas import tpu as pltpu

D_MODEL = 1024
PLE_DIM = 256
SG_GROUPS = 4
SG_CHUNK = 128
SG_WIDTH = D_MODEL // 2
SG_GROUP_DIM = SG_WIDTH // SG_GROUPS
GLA_HEADS = 4
GLA_DK = D_MODEL // 2
GLA_DV = D_MODEL
GLA_HEAD_K = GLA_DK // GLA_HEADS
GLA_HEAD_V = GLA_DV // GLA_HEADS
GLA_GATE_RANK = 16
GLA_GATE_TEMP = 16.0
GLA_CHUNK = 64
D_FF = 4 * D_MODEL
EPS = 1e-6

_C_U = 0
_C_Q = _C_U + 2 * SG_WIDTH
_C_LR = _C_Q + 2 * GLA_DK + GLA_DV
_C_OG = _C_LR + GLA_GATE_RANK
_C_BA = _C_OG + GLA_DV
_C_END = _C_BA + 2 * D_MODEL

MIX_TILE = 512
CUMSUM_ROWS = 256
MIX_PREP_STEPS = 11
BF16_SUBLANES = 16
FFN_TILE = 1024
FFN_SLAB = 1024
VMEM_LIMIT_BYTES = 56 * 1024 * 1024

_BF = jnp.bfloat16
_F32 = jnp.float32

_dot = functools.partial(jnp.dot, preferred_element_type=_F32)


def _rms(x, g):
    ms = jnp.mean(x * x, axis=-1, keepdims=True)
    return x * lax.rsqrt(ms + EPS) * g


def _gelu_tanh(x):
    c = 0.7978845608028654
    return x * (0.5 * (1.0 + jnp.tanh(c * (x + 0.044715 * (x * x * x)))))


def _sigmoid(x):
    return 0.5 * jnp.tanh(0.5 * x) + 0.5


def _log_sigmoid(x):
    return jnp.minimum(x, 0.0) - jnp.log1p(jnp.exp(-jnp.abs(x)))


def _cast_blocks(src_refs, dst_refs):
    for src, dst in zip(src_refs, dst_refs):
        dst[...] = src[...].astype(_BF)


def _mix_kernel(x_ref, g_ref, lmat_ref, w_in_t_f, sg_wout_f, wgu_f, gla_wout_f, wo_f,
                ln_g_ref, ln_b_ref, ws_ref, bs_ref, bgate_ref, gn_g_ref,
                w_up_f, w_down_f, w_pg_f, w_pp_f,
                out_ref, w_up_o, w_down_o, w_pg_o, w_pp_o,
                state_ref, o_scr, w_in_t_ref, sg_wout_ref, wgu_ref, gla_wout_ref, wo_ref, *, n_seq, prep_steps):
    step = pl.program_id(0)

    @pl.when(step < prep_steps)
    def _():
        for src, dst in ((w_in_t_f, w_in_t_ref), (sg_wout_f, sg_wout_ref), (wgu_f, wgu_ref),
                         (gla_wout_f, gla_wout_ref), (wo_f, wo_ref)):
            rows = src.shape[0]
            n_slabs = dst.shape[0] // rows
            if n_slabs == 1:
                dst[...] = src[...].astype(_BF)
            else:
                r0 = pl.multiple_of(jnp.minimum(step, n_slabs - 1) * rows, rows)
                dst[pl.ds(r0, rows), :] = src[...].astype(_BF)

    @pl.when(step >= prep_steps)
    def _():
        _mix_tile((step - prep_steps) % n_seq == 0,
                  x_ref, g_ref, lmat_ref, w_in_t_ref, ln_g_ref, ln_b_ref, ws_ref, bs_ref,
                  sg_wout_ref, wgu_ref, bgate_ref, gn_g_ref, gla_wout_ref, wo_ref,
                  w_up_f, w_down_f, w_pg_f, w_pp_f,
                  out_ref, w_up_o, w_down_o, w_pg_o, w_pp_o, state_ref, o_scr)


def _mix_tile(first_of_sequence, x_ref, g_ref, lmat_ref, w_in_t_ref, ln_g_ref, ln_b_ref, ws_ref, bs_ref,
              sg_wout_ref, wgu_ref, bgate_ref, gn_g_ref, gla_wout_ref, wo_ref,
              w_up_f, w_down_f, w_pg_f, w_pp_f,
              out_ref, w_up_o, w_down_o, w_pg_o, w_pp_o, state_ref, o_scr):
    ts = x_ref.shape[0]

    def proj(lo, hi):
        return lax.dot_general(hb, w_in_t_ref[lo:hi, :], (((1,), (1,)), ((), ())), preferred_element_type=_F32)

    @pl.when(first_of_sequence)
    def _():
        state_ref[...] = jnp.zeros_like(state_ref)

    x = x_ref[...]
    hb = _rms(x, g_ref[...]).astype(_BF)

    glr = proj(_C_LR, _C_OG)
    uv = proj(_C_U, _C_Q)
    zg = _dot(glr.astype(_BF), wgu_ref[...]) + bgate_ref[...]
    qkv = proj(_C_Q, _C_LR)

    log_a = _log_sigmoid(zg) * (1.0 / GLA_GATE_TEMP)
    la_hi = log_a.astype(_BF)
    la_lo = (log_a - la_hi.astype(_F32)).astype(_BF)
    lmat = lmat_ref[...]
    b = jnp.concatenate(
        [_dot(lmat, la_hi[r0:r0 + CUMSUM_ROWS]) + _dot(lmat, la_lo[r0:r0 + CUMSUM_ROWS])
         for r0 in range(0, ts, CUMSUM_ROWS)], axis=0)

    u = _gelu_tanh(uv[:, :SG_WIDTH])
    v = _gelu_tanh(uv[:, SG_WIDTH:])
    mu = jnp.mean(v, axis=-1, keepdims=True)
    vc = v - mu
    var = jnp.mean(vc * vc, axis=-1, keepdims=True)
    vn = (vc * lax.rsqrt(var + EPS) * ln_g_ref[...] + ln_b_ref[...]).astype(_BF)

    og = proj(_C_OG, _C_BA)
    bg = proj(_C_BA, _C_END)

    _cast_blocks((w_up_f, w_down_f, w_pg_f, w_pp_f), (w_up_o, w_down_o, w_pg_o, w_pp_o))

    row = lax.broadcasted_iota(jnp.int32, (SG_CHUNK, SG_CHUNK), 0)
    col = lax.broadcasted_iota(jnp.int32, (SG_CHUNK, SG_CHUNK), 1)
    tril = row >= col
    w_sp = [jnp.where(tril, ws_ref[g], 0.0).astype(_BF) for g in range(SG_GROUPS)]
    bias = bs_ref[...]
    sg_chunks = ts // SG_CHUNK
    per_group = []
    for g in range(SG_GROUPS):
        gs = slice(g * SG_GROUP_DIM, (g + 1) * SG_GROUP_DIM)
        v_wide = jnp.concatenate([vn[c * SG_CHUNK:(c + 1) * SG_CHUNK, gs] for c in range(sg_chunks)], axis=1)
        per_group.append(_dot(w_sp[g], v_wide))
    mixed = jnp.concatenate(
        [jnp.concatenate([per_group[g][:, c * SG_GROUP_DIM:(c + 1) * SG_GROUP_DIM] for g in range(SG_GROUPS)], axis=1)
         + bias for c in range(sg_chunks)], axis=0)
    z = (u * mixed).astype(_BF)
    y_a = _dot(z, sg_wout_ref[...])

    q = qkv[:, :GLA_DK] * (GLA_HEAD_K ** -0.5)
    k = qkv[:, GLA_DK:2 * GLA_DK]
    vb = qkv[:, 2 * GLA_DK:].astype(_BF)
    r64 = lax.broadcasted_iota(jnp.int32, (GLA_CHUNK, GLA_CHUNK), 0)
    c64 = lax.broadcasted_iota(jnp.int32, (GLA_CHUNK, GLA_CHUNK), 1)
    causal = r64 >= c64
    rk = lax.broadcasted_iota(jnp.int32, (GLA_HEAD_K, GLA_HEAD_K), 0)
    ck = lax.broadcasted_iota(jnp.int32, (GLA_HEAD_K, GLA_HEAD_K), 1)
    eye_k = rk == ck
    pair = 2 * GLA_CHUNK
    n_pairs = ts // pair
    no_scores = jnp.zeros((GLA_CHUNK, GLA_CHUNK), _BF)
    contract_last = (((1,), (1,)), ((), ()))
    lhs_l, kv_l, dcol_l = {}, {}, {}
    for m in range(n_pairs):
        sub = []
        for r0 in (m * pair, m * pair + GLA_CHUNK):
            bc = b[r0:r0 + GLA_CHUNK]
            b_mid = bc[GLA_CHUNK // 2 - 1:GLA_CHUNK // 2]
            b_last = bc[GLA_CHUNK - 1:GLA_CHUNK]
            qc = q[r0:r0 + GLA_CHUNK]
            kc = k[r0:r0 + GLA_CHUNK]
            sub.append(dict(
                q_in=(qc * jnp.exp(bc - b_mid)).astype(_BF),
                k_in=(kc * jnp.exp(b_mid - bc)).astype(_BF),
                k_dec=kc * jnp.exp(b_last - bc),
                q_dec=qc * jnp.exp(bc),
                dec=jnp.exp(b_last)))
        ca, cb = sub
        q_dec_a = ca["q_dec"].astype(_BF)
        q_dec_b = cb["q_dec"].astype(_BF)
        q_dec_b_from_a = (cb["q_dec"] * ca["dec"]).astype(_BF)
        k_dec_a = ca["k_dec"].astype(_BF)
        k_pair = jnp.concatenate([(ca["k_dec"] * cb["dec"]).astype(_BF), cb["k_dec"].astype(_BF)], axis=0)
        dec_pair = ca["dec"] * cb["dec"]
        r0 = m * pair
        for hd in range(GLA_HEADS):
            ks = slice(hd * GLA_HEAD_K, (hd + 1) * GLA_HEAD_K)
            vs = slice(hd * GLA_HEAD_V, (hd + 1) * GLA_HEAD_V)
            s_aa = lax.dot_general(ca["q_in"][:, ks], ca["k_in"][:, ks], contract_last, preferred_element_type=_F32)
            s_bb = lax.dot_general(cb["q_in"][:, ks], cb["k_in"][:, ks], contract_last, preferred_element_type=_F32)
            s_ba = lax.dot_general(q_dec_b[:, ks], k_dec_a[:, ks], contract_last, preferred_element_type=_F32)
            s_aa = jnp.where(causal, s_aa, 0.0).astype(_BF)
            s_bb = jnp.where(causal, s_bb, 0.0).astype(_BF)
            lhs_l[m, hd] = jnp.concatenate(
                [jnp.concatenate([q_dec_a[:, ks], s_aa, no_scores], axis=1),
                 jnp.concatenate([q_dec_b_from_a[:, ks], s_ba.astype(_BF), s_bb], axis=1)], axis=0)
            kv_l[m, hd] = lax.dot_general(k_pair[:, ks], vb[r0:r0 + pair, vs], (((0,), (0,)), ((), ())),
                                          preferred_element_type=_F32)
            dcol_l[m, hd] = jnp.sum(
                jnp.where(eye_k, jnp.broadcast_to(dec_pair[:, ks], (GLA_HEAD_K, GLA_HEAD_K)), 0.0),
                axis=1, keepdims=True)

    sprev_l = {}
    for hd in range(GLA_HEADS):
        s_run = state_ref[hd]
        for m in range(n_pairs):
            sprev_l[m, hd] = s_run.astype(_BF)
            s_run = dcol_l[m, hd] * s_run + kv_l[m, hd]
        state_ref[hd] = s_run

    for hd in range(GLA_HEADS):
        for m in range(n_pairs):
            r0 = m * pair
            vs = slice(hd * GLA_HEAD_V, (hd + 1) * GLA_HEAD_V)
            rhs = jnp.concatenate([sprev_l[m, hd], vb[r0:r0 + pair, vs]], axis=0)
            o_scr[r0:r0 + pair, vs] = _dot(lhs_l[m, hd], rhs)
    gn_g = gn_g_ref[...]
    o_heads = []
    for hd in range(GLA_HEADS):
        oh = o_scr[:, hd * GLA_HEAD_V:(hd + 1) * GLA_HEAD_V]
        o_heads.append(_rms(oh, gn_g))
    o_n = jnp.concatenate(o_heads, axis=1)
    o_g = (o_n * (og * _sigmoid(og))).astype(_BF)

    half = ts // 2
    y_b = [_dot(o_g[r0:r0 + half], gla_wout_ref[...]) for r0 in (0, half)]
    for j, r0 in enumerate((0, half)):
        merged = (_sigmoid(bg[r0:r0 + half, :D_MODEL]) * y_a[r0:r0 + half]
                  + _sigmoid(bg[r0:r0 + half, D_MODEL:]) * y_b[j])
        out_ref[r0:r0 + half, :] = x[r0:r0 + half] + _dot(merged.astype(_BF), wo_ref[...])


def _ffn_kernel(x_ref, p_ref, g_ffn_ref, w_up_ref, w_down_ref, g_ple_ref, w_pg_ref, w_pp_ref,
                g_fin_ref, out_ref):
    x = x_ref[...]
    hb = _rms(x, g_ffn_ref[...]).astype(_BF)
    ple = _dot(p_ref[...].astype(_BF), w_pp_ref[...])
    acc = x
    for f0 in range(0, D_FF, FFN_SLAB):
        up = jnp.maximum(_dot(hb, w_up_ref[:, f0:f0 + FFN_SLAB]), 0.0)
        acc = acc + _dot((up * up).astype(_BF), w_down_ref[f0:f0 + FFN_SLAB, :])
    half = x_ref.shape[0] // 2
    for r0 in (0, half):
        acc_h = acc[r0:r0 + half]
        h3 = _rms(acc_h, g_ple_ref[...]).astype(_BF)
        gate = _sigmoid(_dot(h3, w_pg_ref[...]))
        out_ref[r0:r0 + half, :] = _rms(acc_h + gate * ple[r0:r0 + half], g_fin_ref[...])


def _resident(shape):
    zeros = (0,) * len(shape)
    return pl.BlockSpec(shape, lambda *_: zeros, pipeline_mode=pl.Buffered(1))


def _slab_specs(arr, steps, step_of):
    rows = arr.shape[1]
    n = max(k for k in range(1, steps + 1) if rows % k == 0 and (rows // k) % BF16_SUBLANES == 0)
    in_spec = pl.BlockSpec((None, rows // n, arr.shape[2]), lambda *g: (0, jnp.minimum(step_of(*g), n - 1), 0))
    out_spec = pl.BlockSpec((rows // n, arr.shape[2]), lambda *g: (jnp.minimum(step_of(*g), n - 1), 0))
    return in_spec, out_spec


def _bf16_like(arr):
    return jax.ShapeDtypeStruct(arr.shape[1:], _BF)


def kernel(x, p, norm_mix_g, w_in, sg_ln_g, sg_ln_b, sg_w_s, sg_b_s, sg_w_out, gla_w_gate_up, gla_b_gate, gla_norm_g, gla_w_out, w_o, norm_ffn_g, ffn_w_up, ffn_w_down, ple_norm_g, ple_w_gate, ple_w_proj, final_norm_g):
    batch, seq, d = x.shape
    assert w_in.shape[0] == 1
    assert d == D_MODEL and seq % MIX_TILE == 0 and (batch * seq) % FFN_TILE == 0
    tokens = batch * seq
    i = 0

    mix_weights = (jnp.swapaxes(w_in, 1, 2), sg_w_out, gla_w_gate_up, gla_w_out, w_o)
    side_weights = (ffn_w_up, ffn_w_down, ple_w_gate, ple_w_proj)

    bias_sp = jnp.repeat(sg_b_s[i].T, SG_GROUP_DIM, axis=1)
    pos = jnp.arange(CUMSUM_ROWS)
    lmat = ((pos[:, None] // GLA_CHUNK == pos[None, :] // GLA_CHUNK) & (pos[None, :] <= pos[:, None])).astype(_BF)

    n_seq = seq // MIX_TILE
    n_tiles = batch * n_seq

    def tile_of(step):
        return jnp.maximum(step - MIX_PREP_STEPS, 0)

    x_spec = pl.BlockSpec((None, MIX_TILE, D_MODEL), lambda s: (tile_of(s) // n_seq, tile_of(s) % n_seq, 0))
    prep_specs = [_slab_specs(a, MIX_PREP_STEPS, lambda s: s) for a in mix_weights]
    side_specs = [_slab_specs(a, n_tiles, tile_of) for a in side_weights]
    pre = (norm_mix_g[i].reshape(1, D_MODEL), lmat)
    post = (sg_ln_g[i].reshape(1, SG_WIDTH), sg_ln_b[i].reshape(1, SG_WIDTH), sg_w_s[i], bias_sp,
            gla_b_gate[i].reshape(1, GLA_DK), gla_norm_g[i].reshape(1, GLA_HEAD_V))
    x, w_up_b, w_down_b, w_pg_b, w_pp_b = pl.pallas_call(
        functools.partial(_mix_kernel, n_seq=n_seq, prep_steps=MIX_PREP_STEPS),
        grid=(MIX_PREP_STEPS + n_tiles,),
        in_specs=[x_spec] + [_resident(a.shape) for a in pre] + [sp[0] for sp in prep_specs]
        + [_resident(a.shape) for a in post] + [sp[0] for sp in side_specs],
        out_specs=[x_spec] + [sp[1] for sp in side_specs],
        out_shape=[jax.ShapeDtypeStruct((batch, seq, D_MODEL), _F32)] + [_bf16_like(a) for a in side_weights],
        scratch_shapes=[
            pltpu.VMEM((GLA_HEADS, GLA_HEAD_K, GLA_HEAD_V), _F32),
            pltpu.VMEM((MIX_TILE, GLA_DV), _F32),
        ] + [pltpu.VMEM(a.shape[1:], _BF) for a in mix_weights],
        compiler_params=pltpu.CompilerParams(
            dimension_semantics=("arbitrary",),
            vmem_limit_bytes=VMEM_LIMIT_BYTES,
        ),
        name="token_mixing",
    )(x, *pre, *mix_weights, *post, *side_weights)

    ffn_inputs = (
        x.reshape(tokens, D_MODEL),
        p[i].reshape(tokens, PLE_DIM),
        norm_ffn_g[i].reshape(1, D_MODEL),
        w_up_b,
        w_down_b,
        ple_norm_g[i].reshape(1, D_MODEL),
        w_pg_b,
        w_pp_b,
        final_norm_g.reshape(1, D_MODEL),
    )
    tok_spec = pl.BlockSpec((FFN_TILE, D_MODEL), lambda t: (t, 0))
    out = pl.pallas_call(
        _ffn_kernel,
        grid=(tokens // FFN_TILE,),
        in_specs=[tok_spec, pl.BlockSpec((FFN_TILE, PLE_DIM), lambda t: (t, 0))]
        + [_resident(a.shape) for a in ffn_inputs[2:]],
        out_specs=tok_spec,
        out_shape=jax.ShapeDtypeStruct((tokens, D_MODEL), _F32),
        compiler_params=pltpu.CompilerParams(
            dimension_semantics=("arbitrary",),
            vmem_limit_bytes=VMEM_LIMIT_BYTES,
        ),
        name="channel_mixing",
    )(*ffn_inputs)
    return out.reshape(batch, seq, D_MODEL)
```

```python
import functools

import jax
import jax.numpy as jnp
import numpy as np
from jax import lax
from jax.experimental import pallas as pl
from jax.experimental.pallas import tpu as pltpu

D_MODEL = 1024
PLE_DIM = 256
SG_GROUPS = 4
SG_CHUNK = 128
SG_WIDTH = D_MODEL // 2
SG_GROUP_DIM = SG_WIDTH // SG_GROUPS
GLA_HEADS = 4
GLA_DK = D_MODEL // 2
GLA_DV = D_MODEL
GLA_HEAD_K = GLA_DK // GLA_HEADS
GLA_HEAD_V = GLA_DV // GLA_HEADS
GLA_GATE_RANK = 16
GLA_GATE_TEMP = 16.0
GLA_CHUNK = 64
D_FF = 4 * D_MODEL
EPS = 1e-6

_C_U = 0
_C_Q = _C_U + 2 * SG_WIDTH
_C_LR = _C_Q + 2 * GLA_DK + GLA_DV
_C_OG = _C_LR + GLA_GATE_RANK
_C_BA = _C_OG + GLA_DV
_C_END = _C_BA + 2 * D_MODEL

MIX_TILE = 512
CUMSUM_ROWS = 256
MIX_PREP_STEPS = 11
BF16_SUBLANES = 16
FFN_TILE = 1024
FFN_SLAB = 1024
VMEM_LIMIT_BYTES = 56 * 1024 * 1024

_BF = jnp.bfloat16
_F32 = jnp.float32

_dot = functools.partial(jnp.dot, preferred_element_type=_F32)


def _rms(x, g):
    ms = jnp.mean(x * x, axis=-1, keepdims=True)
    return x * lax.rsqrt(ms + EPS) * g


def _gelu_tanh(x):
    c = 0.7978845608028654
    return x * (0.5 * (1.0 + jnp.tanh(c * (x + 0.044715 * (x * x * x)))))


def _sigmoid(x):
    return 0.5 * jnp.tanh(0.5 * x) + 0.5


def _log_sigmoid(x):
    return jnp.minimum(x, 0.0) - jnp.log1p(jnp.exp(-jnp.abs(x)))


def _cast_blocks(src_refs, dst_refs):
    for src, dst in zip(src_refs, dst_refs):
        dst[...] = src[...].astype(_BF)


def _mix_kernel(x_ref, g_ref, lmat_ref, w_in_t_f, sg_wout_f, wgu_f, gla_wout_f, wo_f,
                ln_g_ref, ln_b_ref, ws_ref, bs_ref, bgate_ref, gn_g_ref,
                w_up_f, w_down_f, w_pg_f, w_pp_f,
                out_ref, w_up_o, w_down_o, w_pg_o, w_pp_o,
                state_ref, o_scr, w_in_t_ref, sg_wout_ref, wgu_ref, gla_wout_ref, wo_ref, *, n_seq, prep_steps):
    step = pl.program_id(0)

    @pl.when(step < prep_steps)
    def _():
        for src, dst in ((w_in_t_f, w_in_t_ref), (sg_wout_f, sg_wout_ref), (wgu_f, wgu_ref),
                         (gla_wout_f, gla_wout_ref), (wo_f, wo_ref)):
            rows = src.shape[0]
            n_slabs = dst.shape[0] // rows
            if n_slabs == 1:
                dst[...] = src[...].astype(_BF)
            else:
                r0 = pl.multiple_of(jnp.minimum(step, n_slabs - 1) * rows, rows)
                dst[pl.ds(r0, rows), :] = src[...].astype(_BF)

    @pl.when(step >= prep_steps)
    def _():
        _mix_tile((step - prep_steps) % n_seq == 0,
                  x_ref, g_ref, lmat_ref, w_in_t_ref, ln_g_ref, ln_b_ref, ws_ref, bs_ref,
                  sg_wout_ref, wgu_ref, bgate_ref, gn_g_ref, gla_wout_ref, wo_ref,
                  w_up_f, w_down_f, w_pg_f, w_pp_f,
                  out_ref, w_up_o, w_down_o, w_pg_o, w_pp_o, state_ref, o_scr)


def _mix_tile(first_of_sequence, x_ref, g_ref, lmat_ref, w_in_t_ref, ln_g_ref, ln_b_ref, ws_ref, bs_ref,
              sg_wout_ref, wgu_ref, bgate_ref, gn_g_ref, gla_wout_ref, wo_ref,
              w_up_f, w_down_f, w_pg_f, w_pp_f,
              out_ref, w_up_o, w_down_o, w_pg_o, w_pp_o, state_ref, o_scr):
    ts = x_ref.shape[0]

    def proj(lo, hi):
        return lax.dot_general(hb, w_in_t_ref[lo:hi, :], (((1,), (1,)), ((), ())), preferred_element_type=_F32)

    @pl.when(first_of_sequence)
    def _():
        state_ref[...] = jnp.zeros_like(state_ref)

    x = x_ref[...]
    hb = _rms(x, g_ref[...]).astype(_BF)

    glr = proj(_C_LR, _C_OG)
    uv = proj(_C_U, _C_Q)
    zg = _dot(glr.astype(_BF), wgu_ref[...]) + bgate_ref[...]
    qkv = proj(_C_Q, _C_LR)

    log_a = _log_sigmoid(zg) * (1.0 / GLA_GATE_TEMP)
    la_hi = log_a.astype(_BF)
    la_lo = (log_a - la_hi.astype(_F32)).astype(_BF)
    lmat = lmat_ref[...]
    b = jnp.concatenate(
        [_dot(lmat, la_hi[r0:r0 + CUMSUM_ROWS]) + _dot(lmat, la_lo[r0:r0 + CUMSUM_ROWS])
         for r0 in range(0, ts, CUMSUM_ROWS)], axis=0)

    u = _gelu_tanh(uv[:, :SG_WIDTH])
    v = _gelu_tanh(uv[:, SG_WIDTH:])
    mu = jnp.mean(v, axis=-1, keepdims=True)
    vc = v - mu
    var = jnp.mean(vc * vc, axis=-1, keepdims=True)
    vn = (vc * lax.rsqrt(var + EPS) * ln_g_ref[...] + ln_b_ref[...]).astype(_BF)

    og = proj(_C_OG, _C_BA)
    bg = proj(_C_BA, _C_END)

    _cast_blocks((w_up_f, w_down_f, w_pg_f, w_pp_f), (w_up_o, w_down_o, w_pg_o, w_pp_o))

    row = lax.broadcasted_iota(jnp.int32, (SG_CHUNK, SG_CHUNK), 0)
    col = lax.broadcasted_iota(jnp.int32, (SG_CHUNK, SG_CHUNK), 1)
    tril = row >= col
    w_sp = [jnp.where(tril, ws_ref[g], 0.0).astype(_BF) for g in range(SG_GROUPS)]
    bias = jnp.concatenate(
        [jnp.broadcast_to(
            jnp.sum(jnp.where(row == col, jnp.broadcast_to(bs_ref[g:g + 1, :], (SG_CHUNK, SG_CHUNK)), 0.0),
                    axis=1, keepdims=True), (SG_CHUNK, SG_GROUP_DIM)) for g in range(SG_GROUPS)], axis=1)
    sg_chunks = ts // SG_CHUNK
    per_group = []
    for g in range(SG_GROUPS):
        gs = slice(g * SG_GROUP_DIM, (g + 1) * SG_GROUP_DIM)
        v_wide = jnp.concatenate([vn[c * SG_CHUNK:(c + 1) * SG_CHUNK, gs] for c in range(sg_chunks)], axis=1)
        per_group.append(_dot(w_sp[g], v_wide))
    mixed = jnp.concatenate(
        [jnp.concatenate([per_group[g][:, c * SG_GROUP_DIM:(c + 1) * SG_GROUP_DIM] for g in range(SG_GROUPS)], axis=1)
         + bias for c in range(sg_chunks)], axis=0)
    z = (u * mixed).astype(_BF)
    y_a = _dot(z, sg_wout_ref[...])

    q = qkv[:, :GLA_DK] * (GLA_HEAD_K ** -0.5)
    k = qkv[:, GLA_DK:2 * GLA_DK]
    vb = qkv[:, 2 * GLA_DK:].astype(_BF)
    r64 = lax.broadcasted_iota(jnp.int32, (GLA_CHUNK, GLA_CHUNK), 0)
    c64 = lax.broadcasted_iota(jnp.int32, (GLA_CHUNK, GLA_CHUNK), 1)
    causal = r64 >= c64
    rk = lax.broadcasted_iota(jnp.int32, (GLA_HEAD_K, GLA_HEAD_K), 0)
    ck = lax.broadcasted_iota(jnp.int32, (GLA_HEAD_K, GLA_HEAD_K), 1)
    eye_k = rk == ck
    n_chunks = ts // GLA_CHUNK
    scores_l, kv_l, qdec_l, dcol_l = {}, {}, {}, {}
    for c in range(n_chunks):
        r0 = c * GLA_CHUNK
        bc = b[r0:r0 + GLA_CHUNK]
        b_mid = bc[GLA_CHUNK // 2 - 1:GLA_CHUNK // 2]
        b_last = bc[GLA_CHUNK - 1:GLA_CHUNK]
        qc = q[r0:r0 + GLA_CHUNK]
        kc = k[r0:r0 + GLA_CHUNK]
        q_in = (qc * jnp.exp(bc - b_mid)).astype(_BF)
        k_in = (kc * jnp.exp(b_mid - bc)).astype(_BF)
        k_dec = (kc * jnp.exp(b_last - bc)).astype(_BF)
        qdec_l[c] = (qc * jnp.exp(bc)).astype(_BF)
        dec = jnp.exp(b_last)
        for hd in range(GLA_HEADS):
            ks = slice(hd * GLA_HEAD_K, (hd + 1) * GLA_HEAD_K)
            vs = slice(hd * GLA_HEAD_V, (hd + 1) * GLA_HEAD_V)
            sc = lax.dot_general(q_in[:, ks], k_in[:, ks], (((1,), (1,)), ((), ())),
                                 preferred_element_type=_F32)
            scores_l[c, hd] = jnp.where(causal, sc, 0.0).astype(_BF)
            kv_l[c, hd] = lax.dot_general(k_dec[:, ks], vb[r0:r0 + GLA_CHUNK, vs], (((0,), (0,)), ((), ())),
                                          preferred_element_type=_F32)
            dcol_l[c, hd] = jnp.sum(
                jnp.where(eye_k, jnp.broadcast_to(dec[:, ks], (GLA_HEAD_K, GLA_HEAD_K)), 0.0),
                axis=1, keepdims=True)

    sprev_l = {}
    for hd in range(GLA_HEADS):
        s_run = state_ref[hd]
        for c in range(n_chunks):
            sprev_l[c, hd] = s_run.astype(_BF)
            s_run = dcol_l[c, hd] * s_run + kv_l[c, hd]
        state_ref[hd] = s_run

    for hd in range(GLA_HEADS):
        for c in range(n_chunks):
            r0 = c * GLA_CHUNK
            ks = slice(hd * GLA_HEAD_K, (hd + 1) * GLA_HEAD_K)
            vs = slice(hd * GLA_HEAD_V, (hd + 1) * GLA_HEAD_V)
            lhs = jnp.concatenate([qdec_l[c][:, ks], scores_l[c, hd]], axis=1)
            rhs = jnp.concatenate([sprev_l[c, hd], vb[r0:r0 + GLA_CHUNK, vs]], axis=0)
            o_scr[r0:r0 + GLA_CHUNK, vs] = _dot(lhs, rhs)
    gn_g = gn_g_ref[...]
    o_heads = []
    for hd in range(GLA_HEADS):
        oh = o_scr[:, hd * GLA_HEAD_V:(hd + 1) * GLA_HEAD_V]
        o_heads.append(_rms(oh, gn_g))
    o_n = jnp.concatenate(o_heads, axis=1)
    o_g = (o_n * (og * _sigmoid(og))).astype(_BF)

    half = ts // 2
    y_b = [_dot(o_g[r0:r0 + half], gla_wout_ref[...]) for r0 in (0, half)]
    for j, r0 in enumerate((0, half)):
        merged = (_sigmoid(bg[r0:r0 + half, :D_MODEL]) * y_a[r0:r0 + half]
                  + _sigmoid(bg[r0:r0 + half, D_MODEL:]) * y_b[j])
        out_ref[r0:r0 + half, :] = x[r0:r0 + half] + _dot(merged.astype(_BF), wo_ref[...])


def _ffn_kernel(x_ref, p_ref, g_ffn_ref, w_up_ref, w_down_ref, g_ple_ref, w_pg_ref, w_pp_ref,
                g_fin_ref, out_ref):
    x = x_ref[...]
    hb = _rms(x, g_ffn_ref[...]).astype(_BF)
    ple = _dot(p_ref[...].astype(_BF), w_pp_ref[...])
    acc = x
    for f0 in range(0, D_FF, FFN_SLAB):
        up = jnp.maximum(_dot(hb, w_up_ref[:, f0:f0 + FFN_SLAB]), 0.0)
        acc = acc + _dot((up * up).astype(_BF), w_down_ref[f0:f0 + FFN_SLAB, :])
    half = x_ref.shape[0] // 2
    for r0 in (0, half):
        acc_h = acc[r0:r0 + half]
        h3 = _rms(acc_h, g_ple_ref[...]).astype(_BF)
        gate = _sigmoid(_dot(h3, w_pg_ref[...]))
        out_ref[r0:r0 + half, :] = _rms(acc_h + gate * ple[r0:r0 + half], g_fin_ref[...])


def _resident(shape):
    zeros = (0,) * len(shape)
    return pl.BlockSpec(shape, lambda *_: zeros, pipeline_mode=pl.Buffered(1))


def _slab_specs(arr, steps, step_of):
    rows = arr.shape[1]
    n = max(k for k in range(1, steps + 1) if rows % k == 0 and (rows // k) % BF16_SUBLANES == 0)
    in_spec = pl.BlockSpec((None, rows // n, arr.shape[2]), lambda *g: (0, jnp.minimum(step_of(*g), n - 1), 0))
    out_spec = pl.BlockSpec((rows // n, arr.shape[2]), lambda *g: (jnp.minimum(step_of(*g), n - 1), 0))
    return in_spec, out_spec


def _bf16_like(arr):
    return jax.ShapeDtypeStruct(arr.shape[1:], _BF)


def kernel(x, p, norm_mix_g, w_in, sg_ln_g, sg_ln_b, sg_w_s, sg_b_s, sg_w_out, gla_w_gate_up, gla_b_gate, gla_norm_g, gla_w_out, w_o, norm_ffn_g, ffn_w_up, ffn_w_down, ple_norm_g, ple_w_gate, ple_w_proj, final_norm_g):
    batch, seq, d = x.shape
    assert w_in.shape[0] == 1
    assert d == D_MODEL and seq % MIX_TILE == 0 and (batch * seq) % FFN_TILE == 0
    tokens = batch * seq
    i = 0

    mix_weights = (jnp.swapaxes(w_in, 1, 2), sg_w_out, gla_w_gate_up, gla_w_out, w_o)
    side_weights = (ffn_w_up, ffn_w_down, ple_w_gate, ple_w_proj)

    pos = np.arange(CUMSUM_ROWS)
    lmat = jnp.asarray((pos[:, None] // GLA_CHUNK == pos[None, :] // GLA_CHUNK) & (pos[None, :] <= pos[:, None]), _BF)

    n_seq = seq // MIX_TILE
    n_tiles = batch * n_seq

    def tile_of(step):
        return jnp.maximum(step - MIX_PREP_STEPS, 0)

    x_spec = pl.BlockSpec((None, MIX_TILE, D_MODEL), lambda s: (tile_of(s) // n_seq, tile_of(s) % n_seq, 0))
    prep_specs = [_slab_specs(a, MIX_PREP_STEPS, lambda s: s) for a in mix_weights]
    side_specs = [_slab_specs(a, n_tiles, tile_of) for a in side_weights]
    pre = (norm_mix_g[i].reshape(1, D_MODEL), lmat)
    post = (sg_ln_g[i].reshape(1, SG_WIDTH), sg_ln_b[i].reshape(1, SG_WIDTH), sg_w_s[i], sg_b_s[i],
            gla_b_gate[i].reshape(1, GLA_DK), gla_norm_g[i].reshape(1, GLA_HEAD_V))
    x, w_up_b, w_down_b, w_pg_b, w_pp_b = pl.pallas_call(
        functools.partial(_mix_kernel, n_seq=n_seq, prep_steps=MIX_PREP_STEPS),
        grid=(MIX_PREP_STEPS + n_tiles,),
        in_specs=[x_spec] + [_resident(a.shape) for a in pre] + [sp[0] for sp in prep_specs]
        + [_resident(a.shape) for a in post] + [sp[0] for sp in side_specs],
        out_specs=[x_spec] + [sp[1] for sp in side_specs],
        out_shape=[jax.ShapeDtypeStruct((batch, seq, D_MODEL), _F32)] + [_bf16_like(a) for a in side_weights],
        scratch_shapes=[
            pltpu.VMEM((GLA_HEADS, GLA_HEAD_K, GLA_HEAD_V), _F32),
            pltpu.VMEM((MIX_TILE, GLA_DV), _F32),
        ] + [pltpu.VMEM(a.shape[1:], _BF) for a in mix_weights],
        compiler_params=pltpu.CompilerParams(
            dimension_semantics=("arbitrary",),
            vmem_limit_bytes=VMEM_LIMIT_BYTES,
        ),
        name="token_mixing",
    )(x, *pre, *mix_weights, *post, *side_weights)

    ffn_inputs = (
        x.reshape(tokens, D_MODEL),
        p[i].reshape(tokens, PLE_DIM),
        norm_ffn_g[i].reshape(1, D_MODEL),
        w_up_b,
        w_down_b,
        ple_norm_g[i].reshape(1, D_MODEL),
        w_pg_b,
        w_pp_b,
        final_norm_g.reshape(1, D_MODEL),
    )
    tok_spec = pl.BlockSpec((FFN_TILE, D_MODEL), lambda t: (t, 0))
    out = pl.pallas_call(
        _ffn_kernel,
        grid=(tokens // FFN_TILE,),
        in_specs=[tok_spec, pl.BlockSpec((FFN_TILE, PLE_DIM), lambda t: (t, 0))]
        + [_resident(a.shape) for a in ffn_inputs[2:]],
        out_specs=tok_spec,
        out_shape=jax.ShapeDtypeStruct((tokens, D_MODEL), _F32),
        compiler_params=pltpu.CompilerParams(
            dimension_semantics=("arbitrary",),
            vmem_limit_bytes=VMEM_LIMIT_BYTES,
        ),
        name="channel_mixing",
    )(*ffn_inputs)
    return out.reshape(batch, seq, D_MODEL)
```

```python
import functools

import jax
import jax.numpy as jnp
import numpy as np
from jax import lax
from jax.experimental import pallas as pl
from jax.experimental.pallas import tpu as pltpu

D_MODEL = 1024
PLE_DIM = 256
SG_GROUPS = 4
SG_CHUNK = 128
SG_WIDTH = D_MODEL // 2
SG_GROUP_DIM = SG_WIDTH // SG_GROUPS
GLA_HEADS = 4
GLA_DK = D_MODEL // 2
GLA_DV = D_MODEL
GLA_HEAD_K = GLA_DK // GLA_HEADS
GLA_HEAD_V = GLA_DV // GLA_HEADS
GLA_GATE_RANK = 16
GLA_GATE_TEMP = 16.0
GLA_CHUNK = 64
D_FF = 4 * D_MODEL
EPS = 1e-6

_C_U = 0
_C_Q = _C_U + 2 * SG_WIDTH
_C_LR = _C_Q + 2 * GLA_DK + GLA_DV
_C_OG = _C_LR + GLA_GATE_RANK
_C_BA = _C_OG + GLA_DV
_C_END = _C_BA + 2 * D_MODEL

MIX_TILE = 512
CUMSUM_ROWS = 256
MIX_PREP_STEPS = 7
BF16_SUBLANES = 16
FFN_TILE = 1024
FFN_SLAB = 1024
VMEM_LIMIT_BYTES = 62 * 1024 * 1024

_BF = jnp.bfloat16
_F32 = jnp.float32

_dot = functools.partial(jnp.dot, preferred_element_type=_F32)


def _rms(x, g):
    ms = jnp.mean(x * x, axis=-1, keepdims=True)
    return x * lax.rsqrt(ms + EPS) * g


def _gelu_tanh(x):
    c = 0.7978845608028654
    return x * (0.5 * (1.0 + jnp.tanh(c * (x + 0.044715 * (x * x * x)))))


def _sigmoid(x):
    return 0.5 * jnp.tanh(0.5 * x) + 0.5


def _log_sigmoid(x):
    return jnp.minimum(x, 0.0) - jnp.log1p(jnp.exp(-jnp.abs(x)))


def _cast_blocks(src_refs, dst_refs):
    for src, dst in zip(src_refs, dst_refs):
        dst[...] = src[...].astype(_BF)


def _mix_kernel(x_ref, g_ref, lmat_ref, w_in_t_f, sg_wout_f, wgu_f, gla_wout_f, wo_f,
                ln_g_ref, ln_b_ref, ws_ref, bs_ref, bgate_ref, gn_g_ref,
                w_up_f, w_down_f, w_pg_f, w_pp_f,
                out_ref, w_up_o, w_down_o, w_pg_o, w_pp_o,
                state_ref, o_scr, w_in_t_ref, sg_wout_ref, wgu_ref, gla_wout_ref, wo_ref, *, n_seq, prep_steps):
    step = pl.program_id(0)

    @pl.when(step < prep_steps)
    def _():
        for src, dst in ((w_in_t_f, w_in_t_ref), (sg_wout_f, sg_wout_ref), (wgu_f, wgu_ref),
                         (gla_wout_f, gla_wout_ref), (wo_f, wo_ref)):
            rows = src.shape[0]
            n_slabs = dst.shape[0] // rows
            if n_slabs == 1:
                dst[...] = src[...].astype(_BF)
            else:
                r0 = pl.multiple_of(jnp.minimum(step, n_slabs - 1) * rows, rows)
                dst[pl.ds(r0, rows), :] = src[...].astype(_BF)

    @pl.when(step >= prep_steps)
    def _():
        _mix_tile((step - prep_steps) % n_seq == 0,
                  x_ref, g_ref, lmat_ref, w_in_t_ref, ln_g_ref, ln_b_ref, ws_ref, bs_ref,
                  sg_wout_ref, wgu_ref, bgate_ref, gn_g_ref, gla_wout_ref, wo_ref,
                  w_up_f, w_down_f, w_pg_f, w_pp_f,
                  out_ref, w_up_o, w_down_o, w_pg_o, w_pp_o, state_ref, o_scr)


def _mix_tile(first_of_sequence, x_ref, g_ref, lmat_ref, w_in_t_ref, ln_g_ref, ln_b_ref, ws_ref, bs_ref,
              sg_wout_ref, wgu_ref, bgate_ref, gn_g_ref, gla_wout_ref, wo_ref,
              w_up_f, w_down_f, w_pg_f, w_pp_f,
              out_ref, w_up_o, w_down_o, w_pg_o, w_pp_o, state_ref, o_scr):
    ts = x_ref.shape[0]

    def proj(lo, hi):
        return lax.dot_general(hb, w_in_t_ref[lo:hi, :], (((1,), (1,)), ((), ())), preferred_element_type=_F32)

    @pl.when(first_of_sequence)
    def _():
        state_ref[...] = jnp.zeros_like(state_ref)

    x = x_ref[...]
    hb = _rms(x, g_ref[...]).astype(_BF)

    glr = proj(_C_LR, _C_OG)
    uv = proj(_C_U, _C_Q)
    zg = _dot(glr.astype(_BF), wgu_ref[...]) + bgate_ref[...]
    qkv = proj(_C_Q, _C_LR)

    log_a = _log_sigmoid(zg) * (1.0 / GLA_GATE_TEMP)
    la_hi = log_a.astype(_BF)
    la_lo = (log_a - la_hi.astype(_F32)).astype(_BF)
    lmat = lmat_ref[...]
    b = jnp.concatenate(
        [_dot(lmat, la_hi[r0:r0 + CUMSUM_ROWS]) + _dot(lmat, la_lo[r0:r0 + CUMSUM_ROWS])
         for r0 in range(0, ts, CUMSUM_ROWS)], axis=0)

    u = _gelu_tanh(uv[:, :SG_WIDTH])
    v = _gelu_tanh(uv[:, SG_WIDTH:])
    mu = jnp.mean(v, axis=-1, keepdims=True)
    vc = v - mu
    var = jnp.mean(vc * vc, axis=-1, keepdims=True)
    vn = (vc * lax.rsqrt(var + EPS) * ln_g_ref[...] + ln_b_ref[...]).astype(_BF)

    og = proj(_C_OG, _C_BA)
    bg = proj(_C_BA, _C_END)

    _cast_blocks((w_up_f, w_down_f, w_pg_f, w_pp_f), (w_up_o, w_down_o, w_pg_o, w_pp_o))

    row = lax.broadcasted_iota(jnp.int32, (SG_CHUNK, SG_CHUNK), 0)
    col = lax.broadcasted_iota(jnp.int32, (SG_CHUNK, SG_CHUNK), 1)
    tril = row >= col
    w_sp = [jnp.where(tril, ws_ref[g], 0.0).astype(_BF) for g in range(SG_GROUPS)]
    bias = jnp.concatenate(
        [jnp.broadcast_to(
            jnp.sum(jnp.where(row == col, jnp.broadcast_to(bs_ref[g:g + 1, :], (SG_CHUNK, SG_CHUNK)), 0.0),
                    axis=1, keepdims=True), (SG_CHUNK, SG_GROUP_DIM)) for g in range(SG_GROUPS)], axis=1)
    sg_chunks = ts // SG_CHUNK
    per_group = []
    for g in range(SG_GROUPS):
        gs = slice(g * SG_GROUP_DIM, (g + 1) * SG_GROUP_DIM)
        v_wide = jnp.concatenate([vn[c * SG_CHUNK:(c + 1) * SG_CHUNK, gs] for c in range(sg_chunks)], axis=1)
        per_group.append(_dot(w_sp[g], v_wide))
    mixed = jnp.concatenate(
        [jnp.concatenate([per_group[g][:, c * SG_GROUP_DIM:(c + 1) * SG_GROUP_DIM] for g in range(SG_GROUPS)], axis=1)
         + bias for c in range(sg_chunks)], axis=0)
    z = (u * mixed).astype(_BF)
    y_a = _dot(z, sg_wout_ref[...])

    q = qkv[:, :GLA_DK] * (GLA_HEAD_K ** -0.5)
    k = qkv[:, GLA_DK:2 * GLA_DK]
    vb = qkv[:, 2 * GLA_DK:].astype(_BF)
    r64 = lax.broadcasted_iota(jnp.int32, (GLA_CHUNK, GLA_CHUNK), 0)
    c64 = lax.broadcasted_iota(jnp.int32, (GLA_CHUNK, GLA_CHUNK), 1)
    causal = r64 >= c64
    rk = lax.broadcasted_iota(jnp.int32, (GLA_HEAD_K, GLA_HEAD_K), 0)
    ck = lax.broadcasted_iota(jnp.int32, (GLA_HEAD_K, GLA_HEAD_K), 1)
    eye_k = rk == ck
    n_chunks = ts // GLA_CHUNK
    scores_l, kv_l, qdec_l, dcol_l = {}, {}, {}, {}
    for c in range(n_chunks):
        r0 = c * GLA_CHUNK
        bc = b[r0:r0 + GLA_CHUNK]
        b_mid = bc[GLA_CHUNK // 2 - 1:GLA_CHUNK // 2]
        b_last = bc[GLA_CHUNK - 1:GLA_CHUNK]
        qc = q[r0:r0 + GLA_CHUNK]
        kc = k[r0:r0 + GLA_CHUNK]
        q_in = (qc * jnp.exp(bc - b_mid)).astype(_BF)
        k_in = (kc * jnp.exp(b_mid - bc)).astype(_BF)
        k_dec = (kc * jnp.exp(b_last - bc)).astype(_BF)
        qdec_l[c] = (qc * jnp.exp(bc)).astype(_BF)
        dec = jnp.exp(b_last)
        for hd in range(GLA_HEADS):
            ks = slice(hd * GLA_HEAD_K, (hd + 1) * GLA_HEAD_K)
            vs = slice(hd * GLA_HEAD_V, (hd + 1) * GLA_HEAD_V)
            sc = lax.dot_general(q_in[:, ks], k_in[:, ks], (((1,), (1,)), ((), ())),
                                 preferred_element_type=_F32)
            scores_l[c, hd] = jnp.where(causal, sc, 0.0).astype(_BF)
            kv_l[c, hd] = lax.dot_general(k_dec[:, ks], vb[r0:r0 + GLA_CHUNK, vs], (((0,), (0,)), ((), ())),
                                          preferred_element_type=_F32)
            dcol_l[c, hd] = jnp.sum(
                jnp.where(eye_k, jnp.broadcast_to(dec[:, ks], (GLA_HEAD_K, GLA_HEAD_K)), 0.0),
                axis=1, keepdims=True)

    sprev_l = {}
    for hd in range(GLA_HEADS):
        s_run = state_ref[hd]
        for c in range(n_chunks):
            sprev_l[c, hd] = s_run.astype(_BF)
            s_run = dcol_l[c, hd] * s_run + kv_l[c, hd]
        state_ref[hd] = s_run

    for hd in range(GLA_HEADS):
        for c in range(n_chunks):
            r0 = c * GLA_CHUNK
            ks = slice(hd * GLA_HEAD_K, (hd + 1) * GLA_HEAD_K)
            vs = slice(hd * GLA_HEAD_V, (hd + 1) * GLA_HEAD_V)
            lhs = jnp.concatenate([qdec_l[c][:, ks], scores_l[c, hd]], axis=1)
            rhs = jnp.concatenate([sprev_l[c, hd], vb[r0:r0 + GLA_CHUNK, vs]], axis=0)
            o_scr[r0:r0 + GLA_CHUNK, vs] = _dot(lhs, rhs)
    gn_g = gn_g_ref[...]
    o_heads = []
    for hd in range(GLA_HEADS):
        oh = o_scr[:, hd * GLA_HEAD_V:(hd + 1) * GLA_HEAD_V]
        o_heads.append(_rms(oh, gn_g))
    o_n = jnp.concatenate(o_heads, axis=1)
    o_g = (o_n * (og * _sigmoid(og))).astype(_BF)

    half = ts // 2
    y_b = [_dot(o_g[r0:r0 + half], gla_wout_ref[...]) for r0 in (0, half)]
    for j, r0 in enumerate((0, half)):
        merged = (_sigmoid(bg[r0:r0 + half, :D_MODEL]) * y_a[r0:r0 + half]
                  + _sigmoid(bg[r0:r0 + half, D_MODEL:]) * y_b[j])
        out_ref[r0:r0 + half, :] = x[r0:r0 + half] + _dot(merged.astype(_BF), wo_ref[...])


def _ffn_kernel(x_ref, p_ref, g_ffn_ref, w_up_ref, w_down_ref, g_ple_ref, w_pg_ref, w_pp_ref,
                g_fin_ref, out_ref):
    x = x_ref[...]
    hb = _rms(x, g_ffn_ref[...]).astype(_BF)
    ple = _dot(p_ref[...].astype(_BF), w_pp_ref[...])
    acc = x
    for f0 in range(0, D_FF, FFN_SLAB):
        up = jnp.maximum(_dot(hb, w_up_ref[:, f0:f0 + FFN_SLAB]), 0.0)
        acc = acc + _dot((up * up).astype(_BF), w_down_ref[f0:f0 + FFN_SLAB, :])
    half = x_ref.shape[0] // 2
    for r0 in (0, half):
        acc_h = acc[r0:r0 + half]
        h3 = _rms(acc_h, g_ple_ref[...]).astype(_BF)
        gate = _sigmoid(_dot(h3, w_pg_ref[...]))
        out_ref[r0:r0 + half, :] = _rms(acc_h + gate * ple[r0:r0 + half], g_fin_ref[...])


def _resident(shape):
    zeros = (0,) * len(shape)
    return pl.BlockSpec(shape, lambda *_: zeros, pipeline_mode=pl.Buffered(1))


def _slab_specs(arr, steps, step_of):
    rows = arr.shape[1]
    n = max(k for k in range(1, steps + 1) if rows % k == 0 and (rows // k) % BF16_SUBLANES == 0)
    in_spec = pl.BlockSpec((None, rows // n, arr.shape[2]), lambda *g: (0, jnp.minimum(step_of(*g), n - 1), 0))
    out_spec = pl.BlockSpec((rows // n, arr.shape[2]), lambda *g: (jnp.minimum(step_of(*g), n - 1), 0))
    return in_spec, out_spec


def _bf16_like(arr):
    return jax.ShapeDtypeStruct(arr.shape[1:], _BF)


def kernel(x, p, norm_mix_g, w_in, sg_ln_g, sg_ln_b, sg_w_s, sg_b_s, sg_w_out, gla_w_gate_up, gla_b_gate, gla_norm_g, gla_w_out, w_o, norm_ffn_g, ffn_w_up, ffn_w_down, ple_norm_g, ple_w_gate, ple_w_proj, final_norm_g):
    batch, seq, d = x.shape
    assert w_in.shape[0] == 1
    assert d == D_MODEL and seq % MIX_TILE == 0 and (batch * seq) % FFN_TILE == 0
    tokens = batch * seq
    i = 0

    mix_weights = (jnp.swapaxes(w_in, 1, 2), sg_w_out, gla_w_gate_up, gla_w_out, w_o)
    side_weights = (ffn_w_up, ffn_w_down, ple_w_gate, ple_w_proj)

    pos = np.arange(CUMSUM_ROWS)
    lmat = jnp.asarray((pos[:, None] // GLA_CHUNK == pos[None, :] // GLA_CHUNK) & (pos[None, :] <= pos[:, None]), _BF)

    n_seq = seq // MIX_TILE
    n_tiles = batch * n_seq

    def tile_of(step):
        return jnp.maximum(step - MIX_PREP_STEPS, 0)

    x_spec = pl.BlockSpec((None, MIX_TILE, D_MODEL), lambda s: (tile_of(s) // n_seq, tile_of(s) % n_seq, 0))
    prep_specs = [_slab_specs(a, MIX_PREP_STEPS, lambda s: s) for a in mix_weights]
    side_specs = [_slab_specs(a, n_tiles, tile_of) for a in side_weights]
    pre = (norm_mix_g[i].reshape(1, D_MODEL), lmat)
    post = (sg_ln_g[i].reshape(1, SG_WIDTH), sg_ln_b[i].reshape(1, SG_WIDTH), sg_w_s[i], sg_b_s[i],
            gla_b_gate[i].reshape(1, GLA_DK), gla_norm_g[i].reshape(1, GLA_HEAD_V))
    x, w_up_b, w_down_b, w_pg_b, w_pp_b = pl.pallas_call(
        functools.partial(_mix_kernel, n_seq=n_seq, prep_steps=MIX_PREP_STEPS),
        grid=(MIX_PREP_STEPS + n_tiles,),
        in_specs=[x_spec] + [_resident(a.shape) for a in pre] + [sp[0] for sp in prep_specs]
        + [_resident(a.shape) for a in post] + [sp[0] for sp in side_specs],
        out_specs=[x_spec] + [sp[1] for sp in side_specs],
        out_shape=[jax.ShapeDtypeStruct((batch, seq, D_MODEL), _F32)] + [_bf16_like(a) for a in side_weights],
        scratch_shapes=[
            pltpu.VMEM((GLA_HEADS, GLA_HEAD_K, GLA_HEAD_V), _F32),
            pltpu.VMEM((MIX_TILE, GLA_DV), _F32),
        ] + [pltpu.VMEM(a.shape[1:], _BF) for a in mix_weights],
        compiler_params=pltpu.CompilerParams(
            dimension_semantics=("arbitrary",),
            vmem_limit_bytes=VMEM_LIMIT_BYTES,
        ),
        name="token_mixing",
    )(x, *pre, *mix_weights, *post, *side_weights)

    ffn_inputs = (
        x.reshape(tokens, D_MODEL),
        p[i].reshape(tokens, PLE_DIM),
        norm_ffn_g[i].reshape(1, D_MODEL),
        w_up_b,
        w_down_b,
        ple_norm_g[i].reshape(1, D_MODEL),
        w_pg_b,
        w_pp_b,
        final_norm_g.reshape(1, D_MODEL),
    )
    tok_spec = pl.BlockSpec((FFN_TILE, D_MODEL), lambda t: (t, 0))
    out = pl.pallas_call(
        _ffn_kernel,
        grid=(tokens // FFN_TILE,),
        in_specs=[tok_spec, pl.BlockSpec((FFN_TILE, PLE_DIM), lambda t: (t, 0))]
        + [_resident(a.shape) for a in ffn_inputs[2:]],
        out_specs=tok_spec,
        out_shape=jax.ShapeDtypeStruct((tokens, D_MODEL), _F32),
        compiler_params=pltpu.CompilerParams(
            dimension_semantics=("arbitrary",),
            vmem_limit_bytes=VMEM_LIMIT_BYTES,
        ),
        name="channel_mixing",
    )(*ffn_inputs)
    return out.reshape(batch, seq, D_MODEL)
```

```python
import functools

import jax
import jax.numpy as jnp
import numpy as np
from jax import lax
from jax.experimental import pallas as pl
from jax.experimental.pallas import tpu as pltpu

D_MODEL = 1024
PLE_DIM = 256
SG_GROUPS = 4
SG_CHUNK = 128
SG_WIDTH = D_MODEL // 2
SG_GROUP_DIM = SG_WIDTH // SG_GROUPS
GLA_HEADS = 4
GLA_DK = D_MODEL // 2
GLA_DV = D_MODEL
GLA_HEAD_K = GLA_DK // GLA_HEADS
GLA_HEAD_V = GLA_DV // GLA_HEADS
GLA_GATE_RANK = 16
GLA_GATE_TEMP = 16.0
GLA_CHUNK = 64
D_FF = 4 * D_MODEL
EPS = 1e-6

_C_U = 0
_C_Q = _C_U + 2 * SG_WIDTH
_C_LR = _C_Q + 2 * GLA_DK + GLA_DV
_C_OG = _C_LR + GLA_GATE_RANK
_C_BA = _C_OG + GLA_DV
_C_END = _C_BA + 2 * D_MODEL

MIX_TILE = 512
CUMSUM_ROWS = 256
MIX_PREP_STEPS = 11
BF16_SUBLANES = 16
FFN_TILE = 1024
FFN_SLAB = 1024
VMEM_LIMIT_BYTES = 56 * 1024 * 1024

_BF = jnp.bfloat16
_F32 = jnp.float32

_dot = functools.partial(jnp.dot, preferred_element_type=_F32)


def _rms(x, g):
    ms = jnp.mean(x * x, axis=-1, keepdims=True)
    return x * lax.rsqrt(ms + EPS) * g


def _gelu_tanh(x):
    c = 0.7978845608028654
    return x * (0.5 * (1.0 + jnp.tanh(c * (x + 0.044715 * (x * x * x)))))


def _sigmoid(x):
    return 0.5 * jnp.tanh(0.5 * x) + 0.5


def _log_sigmoid(x):
    return jnp.minimum(x, 0.0) - jnp.log1p(jnp.exp(-jnp.abs(x)))


def _cast_blocks(src_refs, dst_refs):
    for src, dst in zip(src_refs, dst_refs):
        dst[...] = src[...].astype(_BF)


def _mix_kernel(x_ref, g_ref, lmat_ref, w_in_t_f, sg_wout_f, wgu_f, gla_wout_f, wo_f,
                ln_g_ref, ln_b_ref, ws_ref, bs_ref, bgate_ref, gn_g_ref,
                w_up_f, w_down_f, w_pg_f, w_pp_f,
                out_ref, w_up_o, w_down_o, w_pg_o, w_pp_o,
                state_ref, o_scr, w_in_t_ref, sg_wout_ref, wgu_ref, gla_wout_ref, wo_ref, *, n_seq, prep_steps):
    step = pl.program_id(0)

    @pl.when(step < prep_steps)
    def _():
        for src, dst in ((w_in_t_f, w_in_t_ref), (sg_wout_f, sg_wout_ref), (wgu_f, wgu_ref),
                         (gla_wout_f, gla_wout_ref), (wo_f, wo_ref)):
            rows = src.shape[0]
            n_slabs = dst.shape[0] // rows
            if n_slabs == 1:
                dst[...] = src[...].astype(_BF)
            else:
                r0 = pl.multiple_of(jnp.minimum(step, n_slabs - 1) * rows, rows)
                dst[pl.ds(r0, rows), :] = src[...].astype(_BF)

    @pl.when(step >= prep_steps)
    def _():
        _mix_tile((step - prep_steps) % n_seq == 0,
                  x_ref, g_ref, lmat_ref, w_in_t_ref, ln_g_ref, ln_b_ref, ws_ref, bs_ref,
                  sg_wout_ref, wgu_ref, bgate_ref, gn_g_ref, gla_wout_ref, wo_ref,
                  w_up_f, w_down_f, w_pg_f, w_pp_f,
                  out_ref, w_up_o, w_down_o, w_pg_o, w_pp_o, state_ref, o_scr)


def _mix_tile(first_of_sequence, x_ref, g_ref, lmat_ref, w_in_t_ref, ln_g_ref, ln_b_ref, ws_ref, bs_ref,
              sg_wout_ref, wgu_ref, bgate_ref, gn_g_ref, gla_wout_ref, wo_ref,
              w_up_f, w_down_f, w_pg_f, w_pp_f,
              out_ref, w_up_o, w_down_o, w_pg_o, w_pp_o, state_ref, o_scr):
    ts = x_ref.shape[0]

    def proj(lo, hi):
        return lax.dot_general(hb, w_in_t_ref[lo:hi, :], (((1,), (1,)), ((), ())), preferred_element_type=_F32)

    @pl.when(first_of_sequence)
    def _():
        state_ref[...] = jnp.zeros_like(state_ref)

    x = x_ref[...]
    hb = _rms(x, g_ref[...]).astype(_BF)

    glr = proj(_C_LR, _C_OG)
    uv = proj(_C_U, _C_Q)
    zg = _dot(glr.astype(_BF), wgu_ref[...]) + bgate_ref[...]
    qkv = proj(_C_Q, _C_LR)

    log_a = _log_sigmoid(zg) * (1.0 / GLA_GATE_TEMP)
    la_hi = log_a.astype(_BF)
    la_lo = (log_a - la_hi.astype(_F32)).astype(_BF)
    lmat = lmat_ref[...]
    b = jnp.concatenate(
        [_dot(lmat, la_hi[r0:r0 + CUMSUM_ROWS]) + _dot(lmat, la_lo[r0:r0 + CUMSUM_ROWS])
         for r0 in range(0, ts, CUMSUM_ROWS)], axis=0)

    u = _gelu_tanh(uv[:, :SG_WIDTH])
    v = _gelu_tanh(uv[:, SG_WIDTH:])
    mu = jnp.mean(v, axis=-1, keepdims=True)
    vc = v - mu
    var = jnp.mean(vc * vc, axis=-1, keepdims=True)
    vn = (vc * lax.rsqrt(var + EPS) * ln_g_ref[...] + ln_b_ref[...]).astype(_BF)

    og = proj(_C_OG, _C_BA)
    bg = proj(_C_BA, _C_END)

    _cast_blocks((w_up_f, w_down_f, w_pg_f, w_pp_f), (w_up_o, w_down_o, w_pg_o, w_pp_o))

    row = lax.broadcasted_iota(jnp.int32, (SG_CHUNK, SG_CHUNK), 0)
    col = lax.broadcasted_iota(jnp.int32, (SG_CHUNK, SG_CHUNK), 1)
    tril = row >= col
    w_sp = [jnp.where(tril, ws_ref[g], 0.0).astype(_BF) for g in range(SG_GROUPS)]
    bias = jnp.concatenate(
        [jnp.broadcast_to(
            jnp.sum(jnp.where(row == col, jnp.broadcast_to(bs_ref[g:g + 1, :], (SG_CHUNK, SG_CHUNK)), 0.0),
                    axis=1, keepdims=True), (SG_CHUNK, SG_GROUP_DIM)) for g in range(SG_GROUPS)], axis=1)
    sg_chunks = ts // SG_CHUNK
    per_group = []
    for g in range(SG_GROUPS):
        gs = slice(g * SG_GROUP_DIM, (g + 1) * SG_GROUP_DIM)
        v_wide = jnp.concatenate([vn[c * SG_CHUNK:(c + 1) * SG_CHUNK, gs] for c in range(sg_chunks)], axis=1)
        per_group.append(_dot(w_sp[g], v_wide))
    mixed = jnp.concatenate(
        [jnp.concatenate([per_group[g][:, c * SG_GROUP_DIM:(c + 1) * SG_GROUP_DIM] for g in range(SG_GROUPS)], axis=1)
         + bias for c in range(sg_chunks)], axis=0)
    z = (u * mixed).astype(_BF)
    y_a = _dot(z, sg_wout_ref[...])

    q = qkv[:, :GLA_DK] * (GLA_HEAD_K ** -0.5)
    k = qkv[:, GLA_DK:2 * GLA_DK]
    vb = qkv[:, 2 * GLA_DK:].astype(_BF)
    r64 = lax.broadcasted_iota(jnp.int32, (GLA_CHUNK, GLA_CHUNK), 0)
    c64 = lax.broadcasted_iota(jnp.int32, (GLA_CHUNK, GLA_CHUNK), 1)
    causal = r64 >= c64
    rk = lax.broadcasted_iota(jnp.int32, (GLA_HEAD_K, GLA_HEAD_K), 0)
    ck = lax.broadcasted_iota(jnp.int32, (GLA_HEAD_K, GLA_HEAD_K), 1)
    eye_k = rk == ck
    n_chunks = ts // GLA_CHUNK
    scores_l, kv_l, qdec_l, dcol_l = {}, {}, {}, {}
    for c in range(n_chunks):
        r0 = c * GLA_CHUNK
        bc = b[r0:r0 + GLA_CHUNK]
        b_mid = bc[GLA_CHUNK // 2 - 1:GLA_CHUNK // 2]
        b_last = bc[GLA_CHUNK - 1:GLA_CHUNK]
        qc = q[r0:r0 + GLA_CHUNK]
        kc = k[r0:r0 + GLA_CHUNK]
        q_in = (qc * jnp.exp(bc - b_mid)).astype(_BF)
        k_in = (kc * jnp.exp(b_mid - bc)).astype(_BF)
        k_dec = (kc * jnp.exp(b_last - bc)).astype(_BF)
        qdec_l[c] = (qc * jnp.exp(bc)).astype(_BF)
        dec = jnp.exp(b_last)
        for hd in range(GLA_HEADS):
            ks = slice(hd * GLA_HEAD_K, (hd + 1) * GLA_HEAD_K)
            vs = slice(hd * GLA_HEAD_V, (hd + 1) * GLA_HEAD_V)
            sc = lax.dot_general(q_in[:, ks], k_in[:, ks], (((1,), (1,)), ((), ())),
                                 preferred_element_type=_F32)
            scores_l[c, hd] = jnp.where(causal, sc, 0.0).astype(_BF)
            kv_l[c, hd] = lax.dot_general(k_dec[:, ks], vb[r0:r0 + GLA_CHUNK, vs], (((0,), (0,)), ((), ())),
                                          preferred_element_type=_F32)
            dcol_l[c, hd] = jnp.sum(
                jnp.where(eye_k, jnp.broadcast_to(dec[:, ks], (GLA_HEAD_K, GLA_HEAD_K)), 0.0),
                axis=1, keepdims=True)

    sprev_l = {}
    for hd in range(GLA_HEADS):
        s_run = state_ref[hd]
        for c in range(n_chunks):
            sprev_l[c, hd] = s_run.astype(_BF)
            s_run = dcol_l[c, hd] * s_run + kv_l[c, hd]
        state_ref[hd] = s_run

    for hd in range(GLA_HEADS):
        for c in range(n_chunks):
            r0 = c * GLA_CHUNK
            ks = slice(hd * GLA_HEAD_K, (hd + 1) * GLA_HEAD_K)
            vs = slice(hd * GLA_HEAD_V, (hd + 1) * GLA_HEAD_V)
            lhs = jnp.concatenate([qdec_l[c][:, ks], scores_l[c, hd]], axis=1)
            rhs = jnp.concatenate([sprev_l[c, hd], vb[r0:r0 + GLA_CHUNK, vs]], axis=0)
            o_scr[r0:r0 + GLA_CHUNK, vs] = _dot(lhs, rhs)
    gn_g = gn_g_ref[...]
    o_heads = []
    for hd in range(GLA_HEADS):
        oh = o_scr[:, hd * GLA_HEAD_V:(hd + 1) * GLA_HEAD_V]
        o_heads.append(_rms(oh, gn_g))
    o_n = jnp.concatenate(o_heads, axis=1)
    o_g = (o_n * (og * _sigmoid(og))).astype(_BF)

    half = ts // 2
    y_b = [_dot(o_g[r0:r0 + half], gla_wout_ref[...]) for r0 in (0, half)]
    for j, r0 in enumerate((0, half)):
        merged = (_sigmoid(bg[r0:r0 + half, :D_MODEL]) * y_a[r0:r0 + half]
                  + _sigmoid(bg[r0:r0 + half, D_MODEL:]) * y_b[j])
        out_ref[r0:r0 + half, :] = x[r0:r0 + half] + _dot(merged.astype(_BF), wo_ref[...])


def _ffn_kernel(x_ref, p_ref, g_ffn_ref, w_up_ref, w_down_ref, g_ple_ref, w_pg_ref, w_pp_ref,
                g_fin_ref, out_ref):
    x = x_ref[...]
    hb = _rms(x, g_ffn_ref[...]).astype(_BF)
    ple = _dot(p_ref[...].astype(_BF), w_pp_ref[...])
    acc = x
    for f0 in range(0, D_FF, FFN_SLAB):
        up = jnp.maximum(_dot(hb, w_up_ref[:, f0:f0 + FFN_SLAB]), 0.0)
        acc = acc + _dot((up * up).astype(_BF), w_down_ref[f0:f0 + FFN_SLAB, :])
    half = x_ref.shape[0] // 2
    for r0 in (0, half):
        acc_h = acc[r0:r0 + half]
        h3 = _rms(acc_h, g_ple_ref[...]).astype(_BF)
        gate = _sigmoid(_dot(h3, w_pg_ref[...]))
        out_ref[r0:r0 + half, :] = _rms(acc_h + gate * ple[r0:r0 + half], g_fin_ref[...])


def _ffn_loop_kernel(x_hbm, p_hbm, g_ffn_ref, w_up_ref, w_down_ref, g_ple_ref, w_pg_ref, w_pp_ref, g_fin_ref,
                     out_hbm, x_buf, p_buf, o_buf, x_sem, p_sem, o_sem, *, n_tiles):
    tile = x_buf.shape[1]

    def rows(t):
        return pl.ds(pl.multiple_of(t * tile, tile), tile)

    def x_copy(t, slot):
        return pltpu.make_async_copy(x_hbm.at[rows(t), :], x_buf.at[slot], x_sem.at[slot])

    def p_copy(t, slot):
        return pltpu.make_async_copy(p_hbm.at[rows(t), :], p_buf.at[slot], p_sem.at[slot])

    def o_copy(t, slot):
        return pltpu.make_async_copy(o_buf.at[slot], out_hbm.at[rows(t), :], o_sem.at[slot])

    x_copy(0, 0).start()
    p_copy(0, 0).start()

    def step(t, carry):
        slot = t % 2

        @pl.when(t + 1 < n_tiles)
        def _():
            x_copy(t + 1, 1 - slot).start()
            p_copy(t + 1, 1 - slot).start()

        x_copy(t, slot).wait()
        p_copy(t, slot).wait()

        @pl.when(t >= 2)
        def _():
            o_copy(t - 2, slot).wait()

        _ffn_kernel(x_buf.at[slot], p_buf.at[slot], g_ffn_ref, w_up_ref, w_down_ref, g_ple_ref, w_pg_ref, w_pp_ref,
                    g_fin_ref, o_buf.at[slot])
        o_copy(t, slot).start()
        return carry

    lax.fori_loop(0, n_tiles, step, 0)
    o_copy(n_tiles - 2, n_tiles % 2).wait()
    o_copy(n_tiles - 1, (n_tiles - 1) % 2).wait()


def _resident(shape):
    zeros = (0,) * len(shape)
    return pl.BlockSpec(shape, lambda *_: zeros, pipeline_mode=pl.Buffered(1))


def _slab_specs(arr, steps, step_of):
    rows = arr.shape[1]
    n = max(k for k in range(1, steps + 1) if rows % k == 0 and (rows // k) % BF16_SUBLANES == 0)
    in_spec = pl.BlockSpec((None, rows // n, arr.shape[2]), lambda *g: (0, jnp.minimum(step_of(*g), n - 1), 0))
    out_spec = pl.BlockSpec((rows // n, arr.shape[2]), lambda *g: (jnp.minimum(step_of(*g), n - 1), 0))
    return in_spec, out_spec


def _bf16_like(arr):
    return jax.ShapeDtypeStruct(arr.shape[1:], _BF)


def kernel(x, p, norm_mix_g, w_in, sg_ln_g, sg_ln_b, sg_w_s, sg_b_s, sg_w_out, gla_w_gate_up, gla_b_gate, gla_norm_g, gla_w_out, w_o, norm_ffn_g, ffn_w_up, ffn_w_down, ple_norm_g, ple_w_gate, ple_w_proj, final_norm_g):
    batch, seq, d = x.shape
    assert w_in.shape[0] == 1
    assert d == D_MODEL and seq % MIX_TILE == 0 and (batch * seq) % FFN_TILE == 0
    tokens = batch * seq
    i = 0

    mix_weights = (jnp.swapaxes(w_in, 1, 2), sg_w_out, gla_w_gate_up, gla_w_out, w_o)
    side_weights = (ffn_w_up, ffn_w_down, ple_w_gate, ple_w_proj)

    pos = np.arange(CUMSUM_ROWS)
    lmat = jnp.asarray((pos[:, None] // GLA_CHUNK == pos[None, :] // GLA_CHUNK) & (pos[None, :] <= pos[:, None]), _BF)

    n_seq = seq // MIX_TILE
    n_tiles = batch * n_seq

    def tile_of(step):
        return jnp.maximum(step - MIX_PREP_STEPS, 0)

    x_spec = pl.BlockSpec((None, MIX_TILE, D_MODEL), lambda s: (tile_of(s) // n_seq, tile_of(s) % n_seq, 0))
    prep_specs = [_slab_specs(a, MIX_PREP_STEPS, lambda s: s) for a in mix_weights]
    side_specs = [_slab_specs(a, n_tiles, tile_of) for a in side_weights]
    pre = (norm_mix_g[i].reshape(1, D_MODEL), lmat)
    post = (sg_ln_g[i].reshape(1, SG_WIDTH), sg_ln_b[i].reshape(1, SG_WIDTH), sg_w_s[i], sg_b_s[i],
            gla_b_gate[i].reshape(1, GLA_DK), gla_norm_g[i].reshape(1, GLA_HEAD_V))
    x, w_up_b, w_down_b, w_pg_b, w_pp_b = pl.pallas_call(
        functools.partial(_mix_kernel, n_seq=n_seq, prep_steps=MIX_PREP_STEPS),
        grid=(MIX_PREP_STEPS + n_tiles,),
        in_specs=[x_spec] + [_resident(a.shape) for a in pre] + [sp[0] for sp in prep_specs]
        + [_resident(a.shape) for a in post] + [sp[0] for sp in side_specs],
        out_specs=[x_spec] + [sp[1] for sp in side_specs],
        out_shape=[jax.ShapeDtypeStruct((batch, seq, D_MODEL), _F32)] + [_bf16_like(a) for a in side_weights],
        scratch_shapes=[
            pltpu.VMEM((GLA_HEADS, GLA_HEAD_K, GLA_HEAD_V), _F32),
            pltpu.VMEM((MIX_TILE, GLA_DV), _F32),
        ] + [pltpu.VMEM(a.shape[1:], _BF) for a in mix_weights],
        compiler_params=pltpu.CompilerParams(
            dimension_semantics=("arbitrary",),
            vmem_limit_bytes=VMEM_LIMIT_BYTES,
        ),
        name="token_mixing",
    )(x, *pre, *mix_weights, *post, *side_weights)

    ffn_inputs = (
        x.reshape(tokens, D_MODEL),
        p[i].reshape(tokens, PLE_DIM),
        norm_ffn_g[i].reshape(1, D_MODEL),
        w_up_b,
        w_down_b,
        ple_norm_g[i].reshape(1, D_MODEL),
        w_pg_b,
        w_pp_b,
        final_norm_g.reshape(1, D_MODEL),
    )
    n_ffn = tokens // FFN_TILE
    assert n_ffn >= 2
    in_vmem = pl.BlockSpec(memory_space=pltpu.VMEM)
    in_hbm = pl.BlockSpec(memory_space=pl.ANY)
    out = pl.pallas_call(
        functools.partial(_ffn_loop_kernel, n_tiles=n_ffn),
        in_specs=[in_hbm, in_hbm] + [in_vmem] * (len(ffn_inputs) - 2),
        out_specs=in_hbm,
        out_shape=jax.ShapeDtypeStruct((tokens, D_MODEL), _F32),
        scratch_shapes=[
            pltpu.VMEM((2, FFN_TILE, D_MODEL), _F32),
            pltpu.VMEM((2, FFN_TILE, PLE_DIM), _F32),
            pltpu.VMEM((2, FFN_TILE, D_MODEL), _F32),
            pltpu.SemaphoreType.DMA((2,)),
            pltpu.SemaphoreType.DMA((2,)),
            pltpu.SemaphoreType.DMA((2,)),
        ],
        compiler_params=pltpu.CompilerParams(vmem_limit_bytes=VMEM_LIMIT_BYTES),
        name="channel_mixing",
    )(*ffn_inputs)
    return out.reshape(batch, seq, D_MODEL)
```

```python
import functools

import jax
import jax.numpy as jnp
import numpy as np
from jax import lax
from jax.experimental import pallas as pl
from jax.experimental.pallas import tpu as pltpu

D_MODEL = 1024
PLE_DIM = 256
SG_GROUPS = 4
SG_CHUNK = 128
SG_WIDTH = D_MODEL // 2
SG_GROUP_DIM = SG_WIDTH // SG_GROUPS
GLA_HEADS = 4
GLA_DK = D_MODEL // 2
GLA_DV = D_MODEL
GLA_HEAD_K = GLA_DK // GLA_HEADS
GLA_HEAD_V = GLA_DV // GLA_HEADS
GLA_GATE_RANK = 16
GLA_GATE_TEMP = 16.0
GLA_CHUNK = 64
D_FF = 4 * D_MODEL
EPS = 1e-6

_C_U = 0
_C_Q = _C_U + 2 * SG_WIDTH
_C_LR = _C_Q + 2 * GLA_DK + GLA_DV
_C_OG = _C_LR + GLA_GATE_RANK
_C_BA = _C_OG + GLA_DV
_C_END = _C_BA + 2 * D_MODEL

MIX_TILE = 512
CUMSUM_ROWS = 256
MIX_PREP_STEPS = 11
BF16_SUBLANES = 16
X_SLOTS = 3
FFN_TILE = 1024
FFN_SLAB = 1024
VMEM_LIMIT_BYTES = 56 * 1024 * 1024
MIX_VMEM_LIMIT_BYTES = 62 * 1024 * 1024

_BF = jnp.bfloat16
_F32 = jnp.float32

_dot = functools.partial(jnp.dot, preferred_element_type=_F32)


def _rms(x, g):
    ms = jnp.mean(x * x, axis=-1, keepdims=True)
    return x * lax.rsqrt(ms + EPS) * g


def _gelu_tanh(x):
    c = 0.7978845608028654
    return x * (0.5 * (1.0 + jnp.tanh(c * (x + 0.044715 * (x * x * x)))))


def _sigmoid(x):
    return 0.5 * jnp.tanh(0.5 * x) + 0.5


def _log_sigmoid(x):
    return jnp.minimum(x, 0.0) - jnp.log1p(jnp.exp(-jnp.abs(x)))


def _cast_blocks(src_refs, dst_refs):
    for src, dst in zip(src_refs, dst_refs):
        dst[...] = src[...].astype(_BF)


def _mix_kernel(x_hbm, g_ref, lmat_ref, w_in_t_f, sg_wout_f, wgu_f, gla_wout_f, wo_f,
                ln_g_ref, ln_b_ref, ws_ref, bs_ref, bgate_ref, gn_g_ref,
                w_up_f, w_down_f, w_pg_f, w_pp_f,
                out_ref, w_up_o, w_down_o, w_pg_o, w_pp_o,
                state_ref, o_scr, w_in_t_ref, sg_wout_ref, wgu_ref, gla_wout_ref, wo_ref, x_buf, hb_scr, x_sem,
                *, n_seq, n_tiles, prep_steps):
    step = pl.program_id(0)
    tile = step - prep_steps

    def x_copy(t):
        return pltpu.make_async_copy(x_hbm.at[t // n_seq, pl.ds(pl.multiple_of((t % n_seq) * MIX_TILE, MIX_TILE), MIX_TILE), :],
                                     x_buf.at[t % X_SLOTS], x_sem.at[t % X_SLOTS])

    @pl.when(step < prep_steps)
    def _():
        for src, dst in ((w_in_t_f, w_in_t_ref), (sg_wout_f, sg_wout_ref), (wgu_f, wgu_ref),
                         (gla_wout_f, gla_wout_ref), (wo_f, wo_ref)):
            rows = src.shape[0]
            n_slabs = dst.shape[0] // rows
            if n_slabs == 1:
                dst[...] = src[...].astype(_BF)
            else:
                r0 = pl.multiple_of(jnp.minimum(step, n_slabs - 1) * rows, rows)
                dst[pl.ds(r0, rows), :] = src[...].astype(_BF)

    @pl.when(step == prep_steps - 1)
    def _():
        x_copy(0).start()
        x_copy(1).start()
        x_copy(0).wait()
        hb_scr[0] = _rms(x_buf[0], g_ref[...]).astype(_BF)

    @pl.when(step >= prep_steps)
    def _():
        @pl.when(tile + 2 < n_tiles)
        def _():
            x_copy(tile + 2).start()

        @pl.when(tile + 1 < n_tiles)
        def _():
            x_copy(tile + 1).wait()

        _mix_tile(tile % n_seq == 0, x_buf.at[tile % X_SLOTS], x_buf.at[(tile + 1) % X_SLOTS],
                  hb_scr.at[tile % 2], hb_scr.at[(tile + 1) % 2], g_ref, lmat_ref, w_in_t_ref, ln_g_ref, ln_b_ref, ws_ref, bs_ref,
                  sg_wout_ref, wgu_ref, bgate_ref, gn_g_ref, gla_wout_ref, wo_ref,
                  w_up_f, w_down_f, w_pg_f, w_pp_f,
                  out_ref, w_up_o, w_down_o, w_pg_o, w_pp_o, state_ref, o_scr)


def _mix_tile(first_of_sequence, x_ref, xnext_ref, hb_ref, hb_next_ref, g_ref, lmat_ref, w_in_t_ref, ln_g_ref, ln_b_ref, ws_ref, bs_ref,
              sg_wout_ref, wgu_ref, bgate_ref, gn_g_ref, gla_wout_ref, wo_ref,
              w_up_f, w_down_f, w_pg_f, w_pp_f,
              out_ref, w_up_o, w_down_o, w_pg_o, w_pp_o, state_ref, o_scr):
    ts = x_ref.shape[0]

    def proj(lo, hi):
        return lax.dot_general(hb, w_in_t_ref[lo:hi, :], (((1,), (1,)), ((), ())), preferred_element_type=_F32)

    @pl.when(first_of_sequence)
    def _():
        state_ref[...] = jnp.zeros_like(state_ref)

    x = x_ref[...]
    hb = hb_ref[...]
    hb_next_ref[...] = _rms(xnext_ref[...], g_ref[...]).astype(_BF)

    glr = proj(_C_LR, _C_OG)
    uv = proj(_C_U, _C_Q)
    zg = _dot(glr.astype(_BF), wgu_ref[...]) + bgate_ref[...]
    qkv = proj(_C_Q, _C_LR)

    log_a = _log_sigmoid(zg) * (1.0 / GLA_GATE_TEMP)
    la_hi = log_a.astype(_BF)
    la_lo = (log_a - la_hi.astype(_F32)).astype(_BF)
    lmat = lmat_ref[...]
    b = jnp.concatenate(
        [_dot(lmat, la_hi[r0:r0 + CUMSUM_ROWS]) + _dot(lmat, la_lo[r0:r0 + CUMSUM_ROWS])
         for r0 in range(0, ts, CUMSUM_ROWS)], axis=0)

    u = _gelu_tanh(uv[:, :SG_WIDTH])
    v = _gelu_tanh(uv[:, SG_WIDTH:])
    mu = jnp.mean(v, axis=-1, keepdims=True)
    vc = v - mu
    var = jnp.mean(vc * vc, axis=-1, keepdims=True)
    vn = (vc * lax.rsqrt(var + EPS) * ln_g_ref[...] + ln_b_ref[...]).astype(_BF)

    og = proj(_C_OG, _C_BA)
    bg = proj(_C_BA, _C_END)

    _cast_blocks((w_up_f, w_down_f, w_pg_f, w_pp_f), (w_up_o, w_down_o, w_pg_o, w_pp_o))

    row = lax.broadcasted_iota(jnp.int32, (SG_CHUNK, SG_CHUNK), 0)
    col = lax.broadcasted_iota(jnp.int32, (SG_CHUNK, SG_CHUNK), 1)
    tril = row >= col
    w_sp = [jnp.where(tril, ws_ref[g], 0.0).astype(_BF) for g in range(SG_GROUPS)]
    bias = jnp.concatenate(
        [jnp.broadcast_to(
            jnp.sum(jnp.where(row == col, jnp.broadcast_to(bs_ref[g:g + 1, :], (SG_CHUNK, SG_CHUNK)), 0.0),
                    axis=1, keepdims=True), (SG_CHUNK, SG_GROUP_DIM)) for g in range(SG_GROUPS)], axis=1)
    sg_chunks = ts // SG_CHUNK
    per_group = []
    for g in range(SG_GROUPS):
        gs = slice(g * SG_GROUP_DIM, (g + 1) * SG_GROUP_DIM)
        v_wide = jnp.concatenate([vn[c * SG_CHUNK:(c + 1) * SG_CHUNK, gs] for c in range(sg_chunks)], axis=1)
        per_group.append(_dot(w_sp[g], v_wide))
    mixed = jnp.concatenate(
        [jnp.concatenate([per_group[g][:, c * SG_GROUP_DIM:(c + 1) * SG_GROUP_DIM] for g in range(SG_GROUPS)], axis=1)
         + bias for c in range(sg_chunks)], axis=0)
    z = (u * mixed).astype(_BF)
    y_a = _dot(z, sg_wout_ref[...])

    q = qkv[:, :GLA_DK] * (GLA_HEAD_K ** -0.5)
    k = qkv[:, GLA_DK:2 * GLA_DK]
    vb = qkv[:, 2 * GLA_DK:].astype(_BF)
    r64 = lax.broadcasted_iota(jnp.int32, (GLA_CHUNK, GLA_CHUNK), 0)
    c64 = lax.broadcasted_iota(jnp.int32, (GLA_CHUNK, GLA_CHUNK), 1)
    causal = r64 >= c64
    rk = lax.broadcasted_iota(jnp.int32, (GLA_HEAD_K, GLA_HEAD_K), 0)
    ck = lax.broadcasted_iota(jnp.int32, (GLA_HEAD_K, GLA_HEAD_K), 1)
    eye_k = rk == ck
    n_chunks = ts // GLA_CHUNK
    scores_l, kv_l, qdec_l, dcol_l = {}, {}, {}, {}
    for c in range(n_chunks):
        r0 = c * GLA_CHUNK
        bc = b[r0:r0 + GLA_CHUNK]
        b_mid = bc[GLA_CHUNK // 2 - 1:GLA_CHUNK // 2]
        b_last = bc[GLA_CHUNK - 1:GLA_CHUNK]
        qc = q[r0:r0 + GLA_CHUNK]
        kc = k[r0:r0 + GLA_CHUNK]
        q_in = (qc * jnp.exp(bc - b_mid)).astype(_BF)
        k_in = (kc * jnp.exp(b_mid - bc)).astype(_BF)
        k_dec = (kc * jnp.exp(b_last - bc)).astype(_BF)
        qdec_l[c] = (qc * jnp.exp(bc)).astype(_BF)
        dec = jnp.exp(b_last)
        for hd in range(GLA_HEADS):
            ks = slice(hd * GLA_HEAD_K, (hd + 1) * GLA_HEAD_K)
            vs = slice(hd * GLA_HEAD_V, (hd + 1) * GLA_HEAD_V)
            sc = lax.dot_general(q_in[:, ks], k_in[:, ks], (((1,), (1,)), ((), ())),
                                 preferred_element_type=_F32)
            scores_l[c, hd] = jnp.where(causal, sc, 0.0).astype(_BF)
            kv_l[c, hd] = lax.dot_general(k_dec[:, ks], vb[r0:r0 + GLA_CHUNK, vs], (((0,), (0,)), ((), ())),
                                          preferred_element_type=_F32)
            dcol_l[c, hd] = jnp.sum(
                jnp.where(eye_k, jnp.broadcast_to(dec[:, ks], (GLA_HEAD_K, GLA_HEAD_K)), 0.0),
                axis=1, keepdims=True)

    sprev_l = {}
    for hd in range(GLA_HEADS):
        s_run = state_ref[hd]
        for c in range(n_chunks):
            sprev_l[c, hd] = s_run.astype(_BF)
            s_run = dcol_l[c, hd] * s_run + kv_l[c, hd]
        state_ref[hd] = s_run

    for hd in range(GLA_HEADS):
        for c in range(n_chunks):
            r0 = c * GLA_CHUNK
            ks = slice(hd * GLA_HEAD_K, (hd + 1) * GLA_HEAD_K)
            vs = slice(hd * GLA_HEAD_V, (hd + 1) * GLA_HEAD_V)
            lhs = jnp.concatenate([qdec_l[c][:, ks], scores_l[c, hd]], axis=1)
            rhs = jnp.concatenate([sprev_l[c, hd], vb[r0:r0 + GLA_CHUNK, vs]], axis=0)
            o_scr[r0:r0 + GLA_CHUNK, vs] = _dot(lhs, rhs)
    gn_g = gn_g_ref[...]
    o_heads = []
    for hd in range(GLA_HEADS):
        oh = o_scr[:, hd * GLA_HEAD_V:(hd + 1) * GLA_HEAD_V]
        o_heads.append(_rms(oh, gn_g))
    o_n = jnp.concatenate(o_heads, axis=1)
    o_g = (o_n * (og * _sigmoid(og))).astype(_BF)

    half = ts // 2
    y_b = [_dot(o_g[r0:r0 + half], gla_wout_ref[...]) for r0 in (0, half)]
    for j, r0 in enumerate((0, half)):
        merged = (_sigmoid(bg[r0:r0 + half, :D_MODEL]) * y_a[r0:r0 + half]
                  + _sigmoid(bg[r0:r0 + half, D_MODEL:]) * y_b[j])
        out_ref[r0:r0 + half, :] = x[r0:r0 + half] + _dot(merged.astype(_BF), wo_ref[...])


def _ffn_kernel(x_ref, p_ref, g_ffn_ref, w_up_ref, w_down_ref, g_ple_ref, w_pg_ref, w_pp_ref,
                g_fin_ref, out_ref):
    x = x_ref[...]
    hb = _rms(x, g_ffn_ref[...]).astype(_BF)
    ple = _dot(p_ref[...].astype(_BF), w_pp_ref[...])
    acc = x
    for f0 in range(0, D_FF, FFN_SLAB):
        up = jnp.maximum(_dot(hb, w_up_ref[:, f0:f0 + FFN_SLAB]), 0.0)
        acc = acc + _dot((up * up).astype(_BF), w_down_ref[f0:f0 + FFN_SLAB, :])
    half = x_ref.shape[0] // 2
    for r0 in (0, half):
        acc_h = acc[r0:r0 + half]
        h3 = _rms(acc_h, g_ple_ref[...]).astype(_BF)
        gate = _sigmoid(_dot(h3, w_pg_ref[...]))
        out_ref[r0:r0 + half, :] = _rms(acc_h + gate * ple[r0:r0 + half], g_fin_ref[...])


def _resident(shape):
    zeros = (0,) * len(shape)
    return pl.BlockSpec(shape, lambda *_: zeros, pipeline_mode=pl.Buffered(1))


def _slab_specs(arr, steps, step_of):
    rows = arr.shape[1]
    n = max(k for k in range(1, steps + 1) if rows % k == 0 and (rows // k) % BF16_SUBLANES == 0)
    in_spec = pl.BlockSpec((None, rows // n, arr.shape[2]), lambda *g: (0, jnp.minimum(step_of(*g), n - 1), 0))
    out_spec = pl.BlockSpec((rows // n, arr.shape[2]), lambda *g: (jnp.minimum(step_of(*g), n - 1), 0))
    return in_spec, out_spec


def _bf16_like(arr):
    return jax.ShapeDtypeStruct(arr.shape[1:], _BF)


def kernel(x, p, norm_mix_g, w_in, sg_ln_g, sg_ln_b, sg_w_s, sg_b_s, sg_w_out, gla_w_gate_up, gla_b_gate, gla_norm_g, gla_w_out, w_o, norm_ffn_g, ffn_w_up, ffn_w_down, ple_norm_g, ple_w_gate, ple_w_proj, final_norm_g):
    batch, seq, d = x.shape
    assert w_in.shape[0] == 1
    assert d == D_MODEL and seq % MIX_TILE == 0 and (batch * seq) % FFN_TILE == 0 and batch * (seq // MIX_TILE) >= 2
    tokens = batch * seq
    i = 0

    mix_weights = (jnp.swapaxes(w_in, 1, 2), sg_w_out, gla_w_gate_up, gla_w_out, w_o)
    side_weights = (ffn_w_up, ffn_w_down, ple_w_gate, ple_w_proj)

    pos = np.arange(CUMSUM_ROWS)
    lmat = jnp.asarray((pos[:, None] // GLA_CHUNK == pos[None, :] // GLA_CHUNK) & (pos[None, :] <= pos[:, None]), _BF)

    n_seq = seq // MIX_TILE
    n_tiles = batch * n_seq

    def tile_of(step):
        return jnp.maximum(step - MIX_PREP_STEPS, 0)

    x_spec = pl.BlockSpec((None, MIX_TILE, D_MODEL), lambda s: (tile_of(s) // n_seq, tile_of(s) % n_seq, 0))
    prep_specs = [_slab_specs(a, MIX_PREP_STEPS, lambda s: s) for a in mix_weights]
    side_specs = [_slab_specs(a, n_tiles, tile_of) for a in side_weights]
    pre = (norm_mix_g[i].reshape(1, D_MODEL), lmat)
    post = (sg_ln_g[i].reshape(1, SG_WIDTH), sg_ln_b[i].reshape(1, SG_WIDTH), sg_w_s[i], sg_b_s[i],
            gla_b_gate[i].reshape(1, GLA_DK), gla_norm_g[i].reshape(1, GLA_HEAD_V))
    x, w_up_b, w_down_b, w_pg_b, w_pp_b = pl.pallas_call(
        functools.partial(_mix_kernel, n_seq=n_seq, n_tiles=n_tiles, prep_steps=MIX_PREP_STEPS),
        grid=(MIX_PREP_STEPS + n_tiles,),
        in_specs=[pl.BlockSpec(memory_space=pl.ANY)] + [_resident(a.shape) for a in pre] + [sp[0] for sp in prep_specs]
        + [_resident(a.shape) for a in post] + [sp[0] for sp in side_specs],
        out_specs=[x_spec] + [sp[1] for sp in side_specs],
        out_shape=[jax.ShapeDtypeStruct((batch, seq, D_MODEL), _F32)] + [_bf16_like(a) for a in side_weights],
        scratch_shapes=[
            pltpu.VMEM((GLA_HEADS, GLA_HEAD_K, GLA_HEAD_V), _F32),
            pltpu.VMEM((MIX_TILE, GLA_DV), _F32),
        ] + [pltpu.VMEM(a.shape[1:], _BF) for a in mix_weights]
        + [pltpu.VMEM((X_SLOTS, MIX_TILE, D_MODEL), _F32),
           pltpu.VMEM((2, MIX_TILE, D_MODEL), _BF),
           pltpu.SemaphoreType.DMA((X_SLOTS,))],
        compiler_params=pltpu.CompilerParams(
            dimension_semantics=("arbitrary",),
            vmem_limit_bytes=MIX_VMEM_LIMIT_BYTES,
        ),
        name="token_mixing",
    )(x, *pre, *mix_weights, *post, *side_weights)

    ffn_inputs = (
        x.reshape(tokens, D_MODEL),
        p[i].reshape(tokens, PLE_DIM),
        norm_ffn_g[i].reshape(1, D_MODEL),
        w_up_b,
        w_down_b,
        ple_norm_g[i].reshape(1, D_MODEL),
        w_pg_b,
        w_pp_b,
        final_norm_g.reshape(1, D_MODEL),
    )
    tok_spec = pl.BlockSpec((FFN_TILE, D_MODEL), lambda t: (t, 0))
    out = pl.pallas_call(
        _ffn_kernel,
        grid=(tokens // FFN_TILE,),
        in_specs=[tok_spec, pl.BlockSpec((FFN_TILE, PLE_DIM), lambda t: (t, 0))]
        + [_resident(a.shape) for a in ffn_inputs[2:]],
        out_specs=tok_spec,
        out_shape=jax.ShapeDtypeStruct((tokens, D_MODEL), _F32),
        compiler_params=pltpu.CompilerParams(
            dimension_semantics=("arbitrary",),
            vmem_limit_bytes=VMEM_LIMIT_BYTES,
        ),
        name="channel_mixing",
    )(*ffn_inputs)
    return out.reshape(batch, seq, D_MODEL)
```

```python
import functools

import jax
import jax.numpy as jnp
import numpy as np
from jax import lax
from jax.experimental import pallas as pl
from jax.experimental.pallas import tpu as pltpu

D_MODEL = 1024
PLE_DIM = 256
SG_GROUPS = 4
SG_CHUNK = 128
SG_WIDTH = D_MODEL // 2
SG_GROUP_DIM = SG_WIDTH // SG_GROUPS
GLA_HEADS = 4
GLA_DK = D_MODEL // 2
GLA_DV = D_MODEL
GLA_HEAD_K = GLA_DK // GLA_HEADS
GLA_HEAD_V = GLA_DV // GLA_HEADS
GLA_GATE_RANK = 16
GLA_GATE_TEMP = 16.0
GLA_CHUNK = 64
D_FF = 4 * D_MODEL
EPS = 1e-6

_C_U = 0
_C_Q = _C_U + 2 * SG_WIDTH
_C_LR = _C_Q + 2 * GLA_DK + GLA_DV
_C_OG = _C_LR + GLA_GATE_RANK
_C_BA = _C_OG + GLA_DV
_C_END = _C_BA + 2 * D_MODEL

MIX_TILE = 512
CUMSUM_ROWS = 256
MIX_PREP_STEPS = 11
BF16_SUBLANES = 16
FFN_TILE = 1024
FFN_SLAB = 1024
VMEM_LIMIT_BYTES = 56 * 1024 * 1024

_BF = jnp.bfloat16
_F32 = jnp.float32

_dot = functools.partial(jnp.dot, preferred_element_type=_F32)


def _rms(x, g):
    ms = jnp.mean(x * x, axis=-1, keepdims=True)
    return x * lax.rsqrt(ms + EPS) * g


def _gelu_tanh(x):
    c = 0.7978845608028654
    return x * (0.5 * (1.0 + jnp.tanh(c * (x + 0.044715 * (x * x * x)))))


def _sigmoid(x):
    return 0.5 * jnp.tanh(0.5 * x) + 0.5


def _log_sigmoid(x):
    return jnp.minimum(x, 0.0) - jnp.log1p(jnp.exp(-jnp.abs(x)))


def _cast_blocks(src_refs, dst_refs):
    for src, dst in zip(src_refs, dst_refs):
        dst[...] = src[...].astype(_BF)


def _mix_kernel(x_ref, g_ref, lmat_ref, w_in_t_f, sg_wout_f, wgu_f, gla_wout_f, wo_f,
                ln_g_ref, ln_b_ref, ws_ref, bs_ref, bgate_ref, gn_g_ref,
                w_up_f, w_down_f, w_pg_f, w_pp_f,
                out_ref, w_up_o, w_down_o, w_pg_o, w_pp_o,
                state_ref, o_scr, w_in_t_ref, sg_wout_ref, wgu_ref, gla_wout_ref, wo_ref, *, n_seq, prep_steps):
    step = pl.program_id(0)

    @pl.when(step < prep_steps)
    def _():
        for src, dst in ((w_in_t_f, w_in_t_ref), (sg_wout_f, sg_wout_ref), (wgu_f, wgu_ref),
                         (gla_wout_f, gla_wout_ref), (wo_f, wo_ref)):
            rows = src.shape[0]
            n_slabs = dst.shape[0] // rows
            if n_slabs == 1:
                dst[...] = src[...].astype(_BF)
            else:
                r0 = pl.multiple_of(jnp.minimum(step, n_slabs - 1) * rows, rows)
                dst[pl.ds(r0, rows), :] = src[...].astype(_BF)

    @pl.when(step >= prep_steps)
    def _():
        _mix_tile((step - prep_steps) % n_seq == 0,
                  x_ref, g_ref, lmat_ref, w_in_t_ref, ln_g_ref, ln_b_ref, ws_ref, bs_ref,
                  sg_wout_ref, wgu_ref, bgate_ref, gn_g_ref, gla_wout_ref, wo_ref,
                  w_up_f, w_down_f, w_pg_f, w_pp_f,
                  out_ref, w_up_o, w_down_o, w_pg_o, w_pp_o, state_ref, o_scr)


def _mix_tile(first_of_sequence, x_ref, g_ref, lmat_ref, w_in_t_ref, ln_g_ref, ln_b_ref, ws_ref, bs_ref,
              sg_wout_ref, wgu_ref, bgate_ref, gn_g_ref, gla_wout_ref, wo_ref,
              w_up_f, w_down_f, w_pg_f, w_pp_f,
              out_ref, w_up_o, w_down_o, w_pg_o, w_pp_o, state_ref, o_scr):
    ts = x_ref.shape[0]

    def proj(lo, hi):
        return lax.dot_general(hb, w_in_t_ref[lo:hi, :], (((1,), (1,)), ((), ())), preferred_element_type=_F32)

    @pl.when(first_of_sequence)
    def _():
        state_ref[...] = jnp.zeros_like(state_ref)

    x = x_ref[...]
    hb = _rms(x, g_ref[...]).astype(_BF)

    glr = proj(_C_LR, _C_OG)
    uv = proj(_C_U, _C_Q)
    zg = _dot(glr.astype(_BF), wgu_ref[...]) + bgate_ref[...]
    qkv = proj(_C_Q, _C_LR)

    log_a = _log_sigmoid(zg) * (1.0 / GLA_GATE_TEMP)
    la_hi = log_a.astype(_BF)
    la_lo = (log_a - la_hi.astype(_F32)).astype(_BF)
    lmat = lmat_ref[...]
    b = jnp.concatenate(
        [_dot(lmat, la_hi[r0:r0 + CUMSUM_ROWS]) + _dot(lmat, la_lo[r0:r0 + CUMSUM_ROWS])
         for r0 in range(0, ts, CUMSUM_ROWS)], axis=0)

    u = _gelu_tanh(uv[:, :SG_WIDTH])
    v = _gelu_tanh(uv[:, SG_WIDTH:])
    mu = jnp.mean(v, axis=-1, keepdims=True)
    vc = v - mu
    var = jnp.mean(vc * vc, axis=-1, keepdims=True)
    vn = (vc * lax.rsqrt(var + EPS) * ln_g_ref[...] + ln_b_ref[...]).astype(_BF)

    bg = proj(_C_BA, _C_END)

    _cast_blocks((w_up_f, w_down_f, w_pg_f, w_pp_f), (w_up_o, w_down_o, w_pg_o, w_pp_o))

    row = lax.broadcasted_iota(jnp.int32, (SG_CHUNK, SG_CHUNK), 0)
    col = lax.broadcasted_iota(jnp.int32, (SG_CHUNK, SG_CHUNK), 1)
    tril = row >= col
    w_sp = [jnp.where(tril, ws_ref[g], 0.0).astype(_BF) for g in range(SG_GROUPS)]
    bias = jnp.concatenate(
        [jnp.broadcast_to(
            jnp.sum(jnp.where(row == col, jnp.broadcast_to(bs_ref[g:g + 1, :], (SG_CHUNK, SG_CHUNK)), 0.0),
                    axis=1, keepdims=True), (SG_CHUNK, SG_GROUP_DIM)) for g in range(SG_GROUPS)], axis=1)
    sg_chunks = ts // SG_CHUNK
    per_group = []
    for g in range(SG_GROUPS):
        gs = slice(g * SG_GROUP_DIM, (g + 1) * SG_GROUP_DIM)
        v_wide = jnp.concatenate([vn[c * SG_CHUNK:(c + 1) * SG_CHUNK, gs] for c in range(sg_chunks)], axis=1)
        per_group.append(_dot(w_sp[g], v_wide))
    mixed = jnp.concatenate(
        [jnp.concatenate([per_group[g][:, c * SG_GROUP_DIM:(c + 1) * SG_GROUP_DIM] for g in range(SG_GROUPS)], axis=1)
         + bias for c in range(sg_chunks)], axis=0)
    z = (u * mixed).astype(_BF)
    y_a = _dot(z, sg_wout_ref[...])

    q = qkv[:, :GLA_DK] * (GLA_HEAD_K ** -0.5)
    k = qkv[:, GLA_DK:2 * GLA_DK]
    vb = qkv[:, 2 * GLA_DK:].astype(_BF)
    r64 = lax.broadcasted_iota(jnp.int32, (GLA_CHUNK, GLA_CHUNK), 0)
    c64 = lax.broadcasted_iota(jnp.int32, (GLA_CHUNK, GLA_CHUNK), 1)
    causal = r64 >= c64
    rk = lax.broadcasted_iota(jnp.int32, (GLA_HEAD_K, GLA_HEAD_K), 0)
    ck = lax.broadcasted_iota(jnp.int32, (GLA_HEAD_K, GLA_HEAD_K), 1)
    eye_k = rk == ck
    n_chunks = ts // GLA_CHUNK
    scores_l, kv_l, qdec_l, dcol_l = {}, {}, {}, {}
    for c in range(n_chunks):
        r0 = c * GLA_CHUNK
        bc = b[r0:r0 + GLA_CHUNK]
        b_mid = bc[GLA_CHUNK // 2 - 1:GLA_CHUNK // 2]
        b_last = bc[GLA_CHUNK - 1:GLA_CHUNK]
        qc = q[r0:r0 + GLA_CHUNK]
        kc = k[r0:r0 + GLA_CHUNK]
        q_in = (qc * jnp.exp(bc - b_mid)).astype(_BF)
        k_in = (kc * jnp.exp(b_mid - bc)).astype(_BF)
        k_dec = (kc * jnp.exp(b_last - bc)).astype(_BF)
        qdec_l[c] = (qc * jnp.exp(bc)).astype(_BF)
        dec = jnp.exp(b_last)
        for hd in range(GLA_HEADS):
            ks = slice(hd * GLA_HEAD_K, (hd + 1) * GLA_HEAD_K)
            vs = slice(hd * GLA_HEAD_V, (hd + 1) * GLA_HEAD_V)
            sc = lax.dot_general(q_in[:, ks], k_in[:, ks], (((1,), (1,)), ((), ())),
                                 preferred_element_type=_F32)
            scores_l[c, hd] = jnp.where(causal, sc, 0.0).astype(_BF)
            kv_l[c, hd] = lax.dot_general(k_dec[:, ks], vb[r0:r0 + GLA_CHUNK, vs], (((0,), (0,)), ((), ())),
                                          preferred_element_type=_F32)
            dcol_l[c, hd] = jnp.sum(
                jnp.where(eye_k, jnp.broadcast_to(dec[:, ks], (GLA_HEAD_K, GLA_HEAD_K)), 0.0),
                axis=1, keepdims=True)

    sprev_l = {}
    for hd in range(GLA_HEADS):
        s_run = state_ref[hd]
        for c in range(n_chunks):
            sprev_l[c, hd] = s_run.astype(_BF)
            s_run = dcol_l[c, hd] * s_run + kv_l[c, hd]
        state_ref[hd] = s_run

    for hd in range(GLA_HEADS):
        for c in range(n_chunks):
            r0 = c * GLA_CHUNK
            ks = slice(hd * GLA_HEAD_K, (hd + 1) * GLA_HEAD_K)
            vs = slice(hd * GLA_HEAD_V, (hd + 1) * GLA_HEAD_V)
            lhs = jnp.concatenate([qdec_l[c][:, ks], scores_l[c, hd]], axis=1)
            rhs = jnp.concatenate([sprev_l[c, hd], vb[r0:r0 + GLA_CHUNK, vs]], axis=0)
            o_scr[r0:r0 + GLA_CHUNK, vs] = _dot(lhs, rhs)
    og = proj(_C_OG, _C_BA)
    gn_g = gn_g_ref[...]
    o_heads = []
    for hd in range(GLA_HEADS):
        oh = o_scr[:, hd * GLA_HEAD_V:(hd + 1) * GLA_HEAD_V]
        o_heads.append(_rms(oh, gn_g))
    o_n = jnp.concatenate(o_heads, axis=1)
    o_g = (o_n * (og * _sigmoid(og))).astype(_BF)

    half = ts // 2
    y_b = [_dot(o_g[r0:r0 + half], gla_wout_ref[...]) for r0 in (0, half)]
    for j, r0 in enumerate((0, half)):
        merged = (_sigmoid(bg[r0:r0 + half, :D_MODEL]) * y_a[r0:r0 + half]
                  + _sigmoid(bg[r0:r0 + half, D_MODEL:]) * y_b[j])
        out_ref[r0:r0 + half, :] = x[r0:r0 + half] + _dot(merged.astype(_BF), wo_ref[...])


def _ffn_kernel(x_ref, p_ref, g_ffn_ref, w_up_ref, w_down_ref, g_ple_ref, w_pg_ref, w_pp_ref,
                g_fin_ref, out_ref):
    x = x_ref[...]
    hb = _rms(x, g_ffn_ref[...]).astype(_BF)
    ple = _dot(p_ref[...].astype(_BF), w_pp_ref[...])
    acc = x
    for f0 in range(0, D_FF, FFN_SLAB):
        up = jnp.maximum(_dot(hb, w_up_ref[:, f0:f0 + FFN_SLAB]), 0.0)
        acc = acc + _dot((up * up).astype(_BF), w_down_ref[f0:f0 + FFN_SLAB, :])
    half = x_ref.shape[0] // 2
    for r0 in (0, half):
        acc_h = acc[r0:r0 + half]
        h3 = _rms(acc_h, g_ple_ref[...]).astype(_BF)
        gate = _sigmoid(_dot(h3, w_pg_ref[...]))
        out_ref[r0:r0 + half, :] = _rms(acc_h + gate * ple[r0:r0 + half], g_fin_ref[...])


def _resident(shape):
    zeros = (0,) * len(shape)
    return pl.BlockSpec(shape, lambda *_: zeros, pipeline_mode=pl.Buffered(1))


def _slab_specs(arr, steps, step_of):
    rows = arr.shape[1]
    n = max(k for k in range(1, steps + 1) if rows % k == 0 and (rows // k) % BF16_SUBLANES == 0)
    in_spec = pl.BlockSpec((None, rows // n, arr.shape[2]), lambda *g: (0, jnp.minimum(step_of(*g), n - 1), 0))
    out_spec = pl.BlockSpec((rows // n, arr.shape[2]), lambda *g: (jnp.minimum(step_of(*g), n - 1), 0))
    return in_spec, out_spec


def _bf16_like(arr):
    return jax.ShapeDtypeStruct(arr.shape[1:], _BF)


def kernel(x, p, norm_mix_g, w_in, sg_ln_g, sg_ln_b, sg_w_s, sg_b_s, sg_w_out, gla_w_gate_up, gla_b_gate, gla_norm_g, gla_w_out, w_o, norm_ffn_g, ffn_w_up, ffn_w_down, ple_norm_g, ple_w_gate, ple_w_proj, final_norm_g):
    batch, seq, d = x.shape
    assert w_in.shape[0] == 1
    assert d == D_MODEL and seq % MIX_TILE == 0 and (batch * seq) % FFN_TILE == 0
    tokens = batch * seq
    i = 0

    mix_weights = (jnp.swapaxes(w_in, 1, 2), sg_w_out, gla_w_gate_up, gla_w_out, w_o)
    side_weights = (ffn_w_up, ffn_w_down, ple_w_gate, ple_w_proj)

    pos = np.arange(CUMSUM_ROWS)
    lmat = jnp.asarray((pos[:, None] // GLA_CHUNK == pos[None, :] // GLA_CHUNK) & (pos[None, :] <= pos[:, None]), _BF)

    n_seq = seq // MIX_TILE
    n_tiles = batch * n_seq

    def tile_of(step):
        return jnp.maximum(step - MIX_PREP_STEPS, 0)

    x_spec = pl.BlockSpec((None, MIX_TILE, D_MODEL), lambda s: (tile_of(s) // n_seq, tile_of(s) % n_seq, 0))
    prep_specs = [_slab_specs(a, MIX_PREP_STEPS, lambda s: s) for a in mix_weights]
    side_specs = [_slab_specs(a, n_tiles, tile_of) for a in side_weights]
    pre = (norm_mix_g[i].reshape(1, D_MODEL), lmat)
    post = (sg_ln_g[i].reshape(1, SG_WIDTH), sg_ln_b[i].reshape(1, SG_WIDTH), sg_w_s[i], sg_b_s[i],
            gla_b_gate[i].reshape(1, GLA_DK), gla_norm_g[i].reshape(1, GLA_HEAD_V))
    x, w_up_b, w_down_b, w_pg_b, w_pp_b = pl.pallas_call(
        functools.partial(_mix_kernel, n_seq=n_seq, prep_steps=MIX_PREP_STEPS),
        grid=(MIX_PREP_STEPS + n_tiles,),
        in_specs=[x_spec] + [_resident(a.shape) for a in pre] + [sp[0] for sp in prep_specs]
        + [_resident(a.shape) for a in post] + [sp[0] for sp in side_specs],
        out_specs=[x_spec] + [sp[1] for sp in side_specs],
        out_shape=[jax.ShapeDtypeStruct((batch, seq, D_MODEL), _F32)] + [_bf16_like(a) for a in side_weights],
        scratch_shapes=[
            pltpu.VMEM((GLA_HEADS, GLA_HEAD_K, GLA_HEAD_V), _F32),
            pltpu.VMEM((MIX_TILE, GLA_DV), _F32),
        ] + [pltpu.VMEM(a.shape[1:], _BF) for a in mix_weights],
        compiler_params=pltpu.CompilerParams(
            dimension_semantics=("arbitrary",),
            vmem_limit_bytes=VMEM_LIMIT_BYTES,
        ),
        name="token_mixing",
    )(x, *pre, *mix_weights, *post, *side_weights)

    ffn_inputs = (
        x.reshape(tokens, D_MODEL),
        p[i].reshape(tokens, PLE_DIM),
        norm_ffn_g[i].reshape(1, D_MODEL),
        w_up_b,
        w_down_b,
        ple_norm_g[i].reshape(1, D_MODEL),
        w_pg_b,
        w_pp_b,
        final_norm_g.reshape(1, D_MODEL),
    )
    tok_spec = pl.BlockSpec((FFN_TILE, D_MODEL), lambda t: (t, 0))
    out = pl.pallas_call(
        _ffn_kernel,
        grid=(tokens // FFN_TILE,),
        in_specs=[tok_spec, pl.BlockSpec((FFN_TILE, PLE_DIM), lambda t: (t, 0))]
        + [_resident(a.shape) for a in ffn_inputs[2:]],
        out_specs=tok_spec,
        out_shape=jax.ShapeDtypeStruct((tokens, D_MODEL), _F32),
        compiler_params=pltpu.CompilerParams(
            dimension_semantics=("arbitrary",),
            vmem_limit_bytes=VMEM_LIMIT_BYTES,
        ),
        name="channel_mixing",
    )(*ffn_inputs)
    return out.reshape(batch, seq, D_MODEL)
```

```python
import functools

import jax
import jax.numpy as jnp
import numpy as np
from jax import lax
from jax.experimental import pallas as pl
from jax.experimental.pallas import tpu as pltpu

D_MODEL = 1024
PLE_DIM = 256
SG_GROUPS = 4
SG_CHUNK = 128
SG_WIDTH = D_MODEL // 2
SG_GROUP_DIM = SG_WIDTH // SG_GROUPS
GLA_HEADS = 4
GLA_DK = D_MODEL // 2
GLA_DV = D_MODEL
GLA_HEAD_K = GLA_DK // GLA_HEADS
GLA_HEAD_V = GLA_DV // GLA_HEADS
GLA_GATE_RANK = 16
GLA_GATE_TEMP = 16.0
GLA_CHUNK = 64
D_FF = 4 * D_MODEL
EPS = 1e-6

_C_U = 0
_C_Q = _C_U + 2 * SG_WIDTH
_C_LR = _C_Q + 2 * GLA_DK + GLA_DV
_C_OG = _C_LR + GLA_GATE_RANK
_C_BA = _C_OG + GLA_DV
_C_END = _C_BA + 2 * D_MODEL

MIX_TILE = 512
CUMSUM_ROWS = 256
MIX_PREP_STEPS = 11
BF16_SUBLANES = 16
FFN_TILE = 1024
FFN_SLAB = 1024
VMEM_LIMIT_BYTES = 56 * 1024 * 1024

_BF = jnp.bfloat16
_F32 = jnp.float32

_dot = functools.partial(jnp.dot, preferred_element_type=_F32)


def _rms(x, g):
    ms = jnp.mean(x * x, axis=-1, keepdims=True)
    return x * lax.rsqrt(ms + EPS) * g


def _gelu_tanh(x):
    c = 0.7978845608028654
    return x * (0.5 * (1.0 + jnp.tanh(c * (x + 0.044715 * (x * x * x)))))


def _sigmoid(x):
    return 0.5 * jnp.tanh(0.5 * x) + 0.5


def _log_sigmoid(x):
    return jnp.minimum(x, 0.0) - jnp.log1p(jnp.exp(-jnp.abs(x)))


def _cast_blocks(src_refs, dst_refs):
    for src, dst in zip(src_refs, dst_refs):
        dst[...] = src[...].astype(_BF)


def _mix_kernel(x_ref, g_ref, lmat_ref, w_in_t_f, sg_wout_f, wgu_f, gla_wout_f, wo_f,
                ln_g_ref, ln_b_ref, ws_ref, bs_ref, bgate_ref, gn_g_ref,
                w_up_f, w_down_f, w_pg_f, w_pp_f,
                out_ref, w_up_o, w_down_o, w_pg_o, w_pp_o,
                state_ref, o_scr, w_in_t_ref, sg_wout_ref, wgu_ref, gla_wout_ref, wo_ref, *, n_seq, prep_steps):
    step = pl.program_id(0)

    @pl.when(step < prep_steps)
    def _():
        for src, dst in ((w_in_t_f, w_in_t_ref), (sg_wout_f, sg_wout_ref), (wgu_f, wgu_ref),
                         (gla_wout_f, gla_wout_ref), (wo_f, wo_ref)):
            rows = src.shape[0]
            n_slabs = dst.shape[0] // rows
            if n_slabs == 1:
                dst[...] = src[...].astype(_BF)
            else:
                r0 = pl.multiple_of(jnp.minimum(step, n_slabs - 1) * rows, rows)
                dst[pl.ds(r0, rows), :] = src[...].astype(_BF)

    @pl.when(step >= prep_steps)
    def _():
        _mix_tile((step - prep_steps) % n_seq == 0,
                  x_ref, g_ref, lmat_ref, w_in_t_ref, ln_g_ref, ln_b_ref, ws_ref, bs_ref,
                  sg_wout_ref, wgu_ref, bgate_ref, gn_g_ref, gla_wout_ref, wo_ref,
                  w_up_f, w_down_f, w_pg_f, w_pp_f,
                  out_ref, w_up_o, w_down_o, w_pg_o, w_pp_o, state_ref, o_scr)


def _mix_tile(first_of_sequence, x_ref, g_ref, lmat_ref, w_in_t_ref, ln_g_ref, ln_b_ref, ws_ref, bs_ref,
              sg_wout_ref, wgu_ref, bgate_ref, gn_g_ref, gla_wout_ref, wo_ref,
              w_up_f, w_down_f, w_pg_f, w_pp_f,
              out_ref, w_up_o, w_down_o, w_pg_o, w_pp_o, state_ref, o_scr):
    ts = x_ref.shape[0]

    def proj(lo, hi):
        return lax.dot_general(hb, w_in_t_ref[lo:hi, :], (((1,), (1,)), ((), ())), preferred_element_type=_F32)

    @pl.when(first_of_sequence)
    def _():
        state_ref[...] = jnp.zeros_like(state_ref)

    x = x_ref[...]
    hb = _rms(x, g_ref[...]).astype(_BF)

    glr = proj(_C_LR, _C_OG)
    uv = proj(_C_U, _C_Q)
    zg = _dot(glr.astype(_BF), wgu_ref[...]) + bgate_ref[...]
    qkv = proj(_C_Q, _C_LR)

    log_a = _log_sigmoid(zg) * (1.0 / GLA_GATE_TEMP)
    la_hi = log_a.astype(_BF)
    la_lo = (log_a - la_hi.astype(_F32)).astype(_BF)
    lmat = lmat_ref[...]
    b = jnp.concatenate(
        [_dot(lmat, la_hi[r0:r0 + CUMSUM_ROWS]) + _dot(lmat, la_lo[r0:r0 + CUMSUM_ROWS])
         for r0 in range(0, ts, CUMSUM_ROWS)], axis=0)

    u = _gelu_tanh(uv[:, :SG_WIDTH])
    v = _gelu_tanh(uv[:, SG_WIDTH:])
    mu = jnp.mean(v, axis=-1, keepdims=True)
    vc = v - mu
    var = jnp.mean(vc * vc, axis=-1, keepdims=True)
    vn = (vc * lax.rsqrt(var + EPS) * ln_g_ref[...] + ln_b_ref[...]).astype(_BF)

    bg = proj(_C_BA, _C_END)

    _cast_blocks((w_up_f, w_down_f, w_pg_f, w_pp_f), (w_up_o, w_down_o, w_pg_o, w_pp_o))

    row = lax.broadcasted_iota(jnp.int32, (SG_CHUNK, SG_CHUNK), 0)
    col = lax.broadcasted_iota(jnp.int32, (SG_CHUNK, SG_CHUNK), 1)
    tril = row >= col
    w_sp = [jnp.where(tril, ws_ref[g], 0.0).astype(_BF) for g in range(SG_GROUPS)]
    bias = jnp.concatenate(
        [jnp.broadcast_to(
            jnp.sum(jnp.where(row == col, jnp.broadcast_to(bs_ref[g:g + 1, :], (SG_CHUNK, SG_CHUNK)), 0.0),
                    axis=1, keepdims=True), (SG_CHUNK, SG_GROUP_DIM)) for g in range(SG_GROUPS)], axis=1)
    sg_chunks = ts // SG_CHUNK
    per_group = []
    for g in range(SG_GROUPS):
        gs = slice(g * SG_GROUP_DIM, (g + 1) * SG_GROUP_DIM)
        v_wide = jnp.concatenate([vn[c * SG_CHUNK:(c + 1) * SG_CHUNK, gs] for c in range(sg_chunks)], axis=1)
        per_group.append(_dot(w_sp[g], v_wide))
    mixed = jnp.concatenate(
        [jnp.concatenate([per_group[g][:, c * SG_GROUP_DIM:(c + 1) * SG_GROUP_DIM] for g in range(SG_GROUPS)], axis=1)
         + bias for c in range(sg_chunks)], axis=0)
    z = (u * mixed).astype(_BF)
    y_a = _dot(z, sg_wout_ref[...])

    q = qkv[:, :GLA_DK] * (GLA_HEAD_K ** -0.5)
    k = qkv[:, GLA_DK:2 * GLA_DK]
    vb = qkv[:, 2 * GLA_DK:].astype(_BF)
    r64 = lax.broadcasted_iota(jnp.int32, (GLA_CHUNK, GLA_CHUNK), 0)
    c64 = lax.broadcasted_iota(jnp.int32, (GLA_CHUNK, GLA_CHUNK), 1)
    causal = r64 >= c64
    rk = lax.broadcasted_iota(jnp.int32, (GLA_HEAD_K, GLA_HEAD_K), 0)
    ck = lax.broadcasted_iota(jnp.int32, (GLA_HEAD_K, GLA_HEAD_K), 1)
    eye_k = rk == ck
    n_chunks = ts // GLA_CHUNK
    scores_l, kv_l, qdec_l, dcol_l = {}, {}, {}, {}
    for c in range(n_chunks):
        r0 = c * GLA_CHUNK
        bc = b[r0:r0 + GLA_CHUNK]
        b_mid = bc[GLA_CHUNK // 2 - 1:GLA_CHUNK // 2]
        b_last = bc[GLA_CHUNK - 1:GLA_CHUNK]
        qc = q[r0:r0 + GLA_CHUNK]
        kc = k[r0:r0 + GLA_CHUNK]
        q_in = (qc * jnp.exp(bc - b_mid)).astype(_BF)
        k_in = (kc * jnp.exp(b_mid - bc)).astype(_BF)
        k_dec = (kc * jnp.exp(b_last - bc)).astype(_BF)
        qdec_l[c] = (qc * jnp.exp(bc)).astype(_BF)
        dec = jnp.exp(b_last)
        for hd in range(GLA_HEADS):
            ks = slice(hd * GLA_HEAD_K, (hd + 1) * GLA_HEAD_K)
            vs = slice(hd * GLA_HEAD_V, (hd + 1) * GLA_HEAD_V)
            sc = lax.dot_general(q_in[:, ks], k_in[:, ks], (((1,), (1,)), ((), ())),
                                 preferred_element_type=_F32)
            scores_l[c, hd] = jnp.where(causal, sc, 0.0).astype(_BF)
            kv_l[c, hd] = lax.dot_general(k_dec[:, ks], vb[r0:r0 + GLA_CHUNK, vs], (((0,), (0,)), ((), ())),
                                          preferred_element_type=_F32)
            dcol_l[c, hd] = jnp.sum(
                jnp.where(eye_k, jnp.broadcast_to(dec[:, ks], (GLA_HEAD_K, GLA_HEAD_K)), 0.0),
                axis=1, keepdims=True)

    sprev_l = {}
    for hd in range(GLA_HEADS):
        s_run = state_ref[hd]
        for c in range(n_chunks):
            sprev_l[c, hd] = s_run.astype(_BF)
            s_run = dcol_l[c, hd] * s_run + kv_l[c, hd]
        state_ref[hd] = s_run

    for hd in range(GLA_HEADS):
        for c in range(n_chunks):
            r0 = c * GLA_CHUNK
            ks = slice(hd * GLA_HEAD_K, (hd + 1) * GLA_HEAD_K)
            vs = slice(hd * GLA_HEAD_V, (hd + 1) * GLA_HEAD_V)
            lhs = jnp.concatenate([qdec_l[c][:, ks], scores_l[c, hd]], axis=1)
            rhs = jnp.concatenate([sprev_l[c, hd], vb[r0:r0 + GLA_CHUNK, vs]], axis=0)
            o_scr[r0:r0 + GLA_CHUNK, vs] = _dot(lhs, rhs)
    og = proj(_C_OG, _C_BA)
    gn_g = gn_g_ref[...]
    o_heads = []
    for hd in range(GLA_HEADS):
        oh = o_scr[:, hd * GLA_HEAD_V:(hd + 1) * GLA_HEAD_V]
        o_heads.append(_rms(oh, gn_g))
    o_n = jnp.concatenate(o_heads, axis=1)
    o_g = (o_n * (og * _sigmoid(og))).astype(_BF)

    half = ts // 2
    y_b = [_dot(o_g[r0:r0 + half], gla_wout_ref[...]) for r0 in (0, half)]
    for j, r0 in enumerate((0, half)):
        merged = (_sigmoid(bg[r0:r0 + half, :D_MODEL]) * y_a[r0:r0 + half]
                  + _sigmoid(bg[r0:r0 + half, D_MODEL:]) * y_b[j])
        out_ref[r0:r0 + half, :] = x[r0:r0 + half] + _dot(merged.astype(_BF), wo_ref[...])


def _ffn_kernel(x_ref, p_ref, g_ffn_ref, w_up_ref, w_down_ref, g_ple_ref, w_pg_ref, w_pp_ref,
                g_fin_ref, out_ref):
    x = x_ref[...]
    hb = _rms(x, g_ffn_ref[...]).astype(_BF)
    ple = _dot(p_ref[...].astype(_BF), w_pp_ref[...])
    acts = []
    for f0 in range(0, D_FF, FFN_SLAB):
        up = jnp.maximum(_dot(hb, w_up_ref[:, f0:f0 + FFN_SLAB]), 0.0)
        acts.append((up * up).astype(_BF))
    acc = x + _dot(jnp.concatenate(acts, axis=1), w_down_ref[...])
    half = x_ref.shape[0] // 2
    for r0 in (0, half):
        acc_h = acc[r0:r0 + half]
        h3 = _rms(acc_h, g_ple_ref[...]).astype(_BF)
        gate = _sigmoid(_dot(h3, w_pg_ref[...]))
        out_ref[r0:r0 + half, :] = _rms(acc_h + gate * ple[r0:r0 + half], g_fin_ref[...])


def _resident(shape):
    zeros = (0,) * len(shape)
    return pl.BlockSpec(shape, lambda *_: zeros, pipeline_mode=pl.Buffered(1))


def _slab_specs(arr, steps, step_of):
    rows = arr.shape[1]
    n = max(k for k in range(1, steps + 1) if rows % k == 0 and (rows // k) % BF16_SUBLANES == 0)
    in_spec = pl.BlockSpec((None, rows // n, arr.shape[2]), lambda *g: (0, jnp.minimum(step_of(*g), n - 1), 0))
    out_spec = pl.BlockSpec((rows // n, arr.shape[2]), lambda *g: (jnp.minimum(step_of(*g), n - 1), 0))
    return in_spec, out_spec


def _bf16_like(arr):
    return jax.ShapeDtypeStruct(arr.shape[1:], _BF)


def kernel(x, p, norm_mix_g, w_in, sg_ln_g, sg_ln_b, sg_w_s, sg_b_s, sg_w_out, gla_w_gate_up, gla_b_gate, gla_norm_g, gla_w_out, w_o, norm_ffn_g, ffn_w_up, ffn_w_down, ple_norm_g, ple_w_gate, ple_w_proj, final_norm_g):
    batch, seq, d = x.shape
    assert w_in.shape[0] == 1
    assert d == D_MODEL and seq % MIX_TILE == 0 and (batch * seq) % FFN_TILE == 0
    tokens = batch * seq
    i = 0

    mix_weights = (jnp.swapaxes(w_in, 1, 2), sg_w_out, gla_w_gate_up, gla_w_out, w_o)
    side_weights = (ffn_w_up, ffn_w_down, ple_w_gate, ple_w_proj)

    pos = np.arange(CUMSUM_ROWS)
    lmat = jnp.asarray((pos[:, None] // GLA_CHUNK == pos[None, :] // GLA_CHUNK) & (pos[None, :] <= pos[:, None]), _BF)

    n_seq = seq // MIX_TILE
    n_tiles = batch * n_seq

    def tile_of(step):
        return jnp.maximum(step - MIX_PREP_STEPS, 0)

    x_spec = pl.BlockSpec((None, MIX_TILE, D_MODEL), lambda s: (tile_of(s) // n_seq, tile_of(s) % n_seq, 0))
    prep_specs = [_slab_specs(a, MIX_PREP_STEPS, lambda s: s) for a in mix_weights]
    side_specs = [_slab_specs(a, n_tiles, tile_of) for a in side_weights]
    pre = (norm_mix_g[i].reshape(1, D_MODEL), lmat)
    post = (sg_ln_g[i].reshape(1, SG_WIDTH), sg_ln_b[i].reshape(1, SG_WIDTH), sg_w_s[i], sg_b_s[i],
            gla_b_gate[i].reshape(1, GLA_DK), gla_norm_g[i].reshape(1, GLA_HEAD_V))
    x, w_up_b, w_down_b, w_pg_b, w_pp_b = pl.pallas_call(
        functools.partial(_mix_kernel, n_seq=n_seq, prep_steps=MIX_PREP_STEPS),
        grid=(MIX_PREP_STEPS + n_tiles,),
        in_specs=[x_spec] + [_resident(a.shape) for a in pre] + [sp[0] for sp in prep_specs]
        + [_resident(a.shape) for a in post] + [sp[0] for sp in side_specs],
        out_specs=[x_spec] + [sp[1] for sp in side_specs],
        out_shape=[jax.ShapeDtypeStruct((batch, seq, D_MODEL), _F32)] + [_bf16_like(a) for a in side_weights],
        scratch_shapes=[
            pltpu.VMEM((GLA_HEADS, GLA_HEAD_K, GLA_HEAD_V), _F32),
            pltpu.VMEM((MIX_TILE, GLA_DV), _F32),
        ] + [pltpu.VMEM(a.shape[1:], _BF) for a in mix_weights],
        compiler_params=pltpu.CompilerParams(
            dimension_semantics=("arbitrary",),
            vmem_limit_bytes=VMEM_LIMIT_BYTES,
        ),
        name="token_mixing",
    )(x, *pre, *mix_weights, *post, *side_weights)

    ffn_inputs = (
        x.reshape(tokens, D_MODEL),
        p[i].reshape(tokens, PLE_DIM),
        norm_ffn_g[i].reshape(1, D_MODEL),
        w_up_b,
        w_down_b,
        ple_norm_g[i].reshape(1, D_MODEL),
        w_pg_b,
        w_pp_b,
        final_norm_g.reshape(1, D_MODEL),
    )
    tok_spec = pl.BlockSpec((FFN_TILE, D_MODEL), lambda t: (t, 0))
    out = pl.pallas_call(
        _ffn_kernel,
        grid=(tokens // FFN_TILE,),
        in_specs=[tok_spec, pl.BlockSpec((FFN_TILE, PLE_DIM), lambda t: (t, 0))]
        + [_resident(a.shape) for a in ffn_inputs[2:]],
        out_specs=tok_spec,
        out_shape=jax.ShapeDtypeStruct((tokens, D_MODEL), _F32),
        compiler_params=pltpu.CompilerParams(
            dimension_semantics=("arbitrary",),
            vmem_limit_bytes=VMEM_LIMIT_BYTES,
        ),
        name="channel_mixing",
    )(*ffn_inputs)
    return out.reshape(batch, seq, D_MODEL)
```

```python
import functools

import jax
import jax.numpy as jnp
import numpy as np
from jax import lax
from jax.experimental import pallas as pl
from jax.experimental.pallas import tpu as pltpu

D_MODEL = 1024
PLE_DIM = 256
SG_GROUPS = 4
SG_CHUNK = 128
SG_WIDTH = D_MODEL // 2
SG_GROUP_DIM = SG_WIDTH // SG_GROUPS
GLA_HEADS = 4
GLA_DK = D_MODEL // 2
GLA_DV = D_MODEL
GLA_HEAD_K = GLA_DK // GLA_HEADS
GLA_HEAD_V = GLA_DV // GLA_HEADS
GLA_GATE_RANK = 16
GLA_GATE_TEMP = 16.0
GLA_CHUNK = 64
D_FF = 4 * D_MODEL
EPS = 1e-6

_C_U = 0
_C_Q = _C_U + 2 * SG_WIDTH
_C_LR = _C_Q + 2 * GLA_DK + GLA_DV
_C_OG = _C_LR + GLA_GATE_RANK
_C_BA = _C_OG + GLA_DV
_C_END = _C_BA + 2 * D_MODEL

MIX_TILE = 512
CUMSUM_ROWS = 256
MIX_PREP_STEPS = 11
BF16_SUBLANES = 16
FFN_TILE = 1024
FFN_SLAB = 1024
VMEM_LIMIT_BYTES = 56 * 1024 * 1024

_BF = jnp.bfloat16
_F32 = jnp.float32

_dot = functools.partial(jnp.dot, preferred_element_type=_F32)


def _rms(x, g):
    ms = jnp.mean(x * x, axis=-1, keepdims=True)
    return x * lax.rsqrt(ms + EPS) * g


def _gelu_tanh(x):
    c = 0.7978845608028654
    return x * (0.5 * (1.0 + jnp.tanh(c * (x + 0.044715 * (x * x * x)))))


def _sigmoid(x):
    return 0.5 * jnp.tanh(0.5 * x) + 0.5


def _log_sigmoid(x):
    return jnp.minimum(x, 0.0) - jnp.log1p(jnp.exp(-jnp.abs(x)))


def _cast_blocks(src_refs, dst_refs):
    for src, dst in zip(src_refs, dst_refs):
        dst[...] = src[...].astype(_BF)


def _mix_kernel(x_ref, g_ref, lmat_ref, w_in_t_f, sg_wout_f, wgu_f, gla_wout_f, wo_f,
                ln_g_ref, ln_b_ref, ws_ref, bs_ref, bgate_ref, gn_g_ref,
                w_up_f, w_down_f, w_pg_f, w_pp_f,
                out_ref, w_up_o, w_down_o, w_pg_o, w_pp_o,
                state_ref, o_scr, w_in_t_ref, sg_wout_ref, wgu_ref, gla_wout_ref, wo_ref, *, n_seq, prep_steps):
    step = pl.program_id(0)

    @pl.when(step < prep_steps)
    def _():
        for src, dst in ((w_in_t_f, w_in_t_ref), (sg_wout_f, sg_wout_ref), (wgu_f, wgu_ref),
                         (gla_wout_f, gla_wout_ref), (wo_f, wo_ref)):
            rows = src.shape[0]
            n_slabs = dst.shape[0] // rows
            if n_slabs == 1:
                dst[...] = src[...].astype(_BF)
            else:
                r0 = pl.multiple_of(jnp.minimum(step, n_slabs - 1) * rows, rows)
                dst[pl.ds(r0, rows), :] = src[...].astype(_BF)

    @pl.when(step >= prep_steps)
    def _():
        _mix_tile((step - prep_steps) % n_seq == 0,
                  x_ref, g_ref, lmat_ref, w_in_t_ref, ln_g_ref, ln_b_ref, ws_ref, bs_ref,
                  sg_wout_ref, wgu_ref, bgate_ref, gn_g_ref, gla_wout_ref, wo_ref,
                  w_up_f, w_down_f, w_pg_f, w_pp_f,
                  out_ref, w_up_o, w_down_o, w_pg_o, w_pp_o, state_ref, o_scr)


def _mix_tile(first_of_sequence, x_ref, g_ref, lmat_ref, w_in_t_ref, ln_g_ref, ln_b_ref, ws_ref, bs_ref,
              sg_wout_ref, wgu_ref, bgate_ref, gn_g_ref, gla_wout_ref, wo_ref,
              w_up_f, w_down_f, w_pg_f, w_pp_f,
              out_ref, w_up_o, w_down_o, w_pg_o, w_pp_o, state_ref, o_scr):
    ts = x_ref.shape[0]

    def proj(lo, hi):
        return lax.dot_general(hb, w_in_t_ref[lo:hi, :], (((1,), (1,)), ((), ())), preferred_element_type=_F32)

    @pl.when(first_of_sequence)
    def _():
        state_ref[...] = jnp.zeros_like(state_ref)

    x = x_ref[...]
    hb = _rms(x, g_ref[...]).astype(_BF)

    glr = proj(_C_LR, _C_OG)
    uv = proj(_C_U, _C_Q)
    zg = _dot(glr.astype(_BF), wgu_ref[...]) + bgate_ref[...]
    qkv = proj(_C_Q, _C_LR)

    log_a = _log_sigmoid(zg) * (1.0 / GLA_GATE_TEMP)
    la_hi = log_a.astype(_BF)
    la_lo = (log_a - la_hi.astype(_F32)).astype(_BF)
    lmat2 = jnp.concatenate([lmat_ref[...], lmat_ref[...]], axis=1)
    b = jnp.concatenate(
        [_dot(lmat2, jnp.concatenate([la_hi[r0:r0 + CUMSUM_ROWS], la_lo[r0:r0 + CUMSUM_ROWS]], axis=0))
         for r0 in range(0, ts, CUMSUM_ROWS)], axis=0)

    u = _gelu_tanh(uv[:, :SG_WIDTH])
    v = _gelu_tanh(uv[:, SG_WIDTH:])
    mu = jnp.mean(v, axis=-1, keepdims=True)
    vc = v - mu
    var = jnp.mean(vc * vc, axis=-1, keepdims=True)
    vn = (vc * lax.rsqrt(var + EPS) * ln_g_ref[...] + ln_b_ref[...]).astype(_BF)

    bg = proj(_C_BA, _C_END)

    _cast_blocks((w_up_f, w_down_f, w_pg_f, w_pp_f), (w_up_o, w_down_o, w_pg_o, w_pp_o))

    row = lax.broadcasted_iota(jnp.int32, (SG_CHUNK, SG_CHUNK), 0)
    col = lax.broadcasted_iota(jnp.int32, (SG_CHUNK, SG_CHUNK), 1)
    tril = row >= col
    w_sp = [jnp.where(tril, ws_ref[g], 0.0).astype(_BF) for g in range(SG_GROUPS)]
    bias = jnp.concatenate(
        [jnp.broadcast_to(
            jnp.sum(jnp.where(row == col, jnp.broadcast_to(bs_ref[g:g + 1, :], (SG_CHUNK, SG_CHUNK)), 0.0),
                    axis=1, keepdims=True), (SG_CHUNK, SG_GROUP_DIM)) for g in range(SG_GROUPS)], axis=1)
    sg_chunks = ts // SG_CHUNK
    per_group = []
    for g in range(SG_GROUPS):
        gs = slice(g * SG_GROUP_DIM, (g + 1) * SG_GROUP_DIM)
        v_wide = jnp.concatenate([vn[c * SG_CHUNK:(c + 1) * SG_CHUNK, gs] for c in range(sg_chunks)], axis=1)
        per_group.append(_dot(w_sp[g], v_wide))
    mixed = jnp.concatenate(
        [jnp.concatenate([per_group[g][:, c * SG_GROUP_DIM:(c + 1) * SG_GROUP_DIM] for g in range(SG_GROUPS)], axis=1)
         + bias for c in range(sg_chunks)], axis=0)
    z = (u * mixed).astype(_BF)
    y_a = _dot(z, sg_wout_ref[...])

    q = qkv[:, :GLA_DK] * (GLA_HEAD_K ** -0.5)
    k = qkv[:, GLA_DK:2 * GLA_DK]
    vb = qkv[:, 2 * GLA_DK:].astype(_BF)
    r64 = lax.broadcasted_iota(jnp.int32, (GLA_CHUNK, GLA_CHUNK), 0)
    c64 = lax.broadcasted_iota(jnp.int32, (GLA_CHUNK, GLA_CHUNK), 1)
    causal = r64 >= c64
    rk = lax.broadcasted_iota(jnp.int32, (GLA_HEAD_K, GLA_HEAD_K), 0)
    ck = lax.broadcasted_iota(jnp.int32, (GLA_HEAD_K, GLA_HEAD_K), 1)
    eye_k = rk == ck
    n_chunks = ts // GLA_CHUNK
    scores_l, kv_l, qdec_l, dcol_l = {}, {}, {}, {}
    for c in range(n_chunks):
        r0 = c * GLA_CHUNK
        bc = b[r0:r0 + GLA_CHUNK]
        b_mid = bc[GLA_CHUNK // 2 - 1:GLA_CHUNK // 2]
        b_last = bc[GLA_CHUNK - 1:GLA_CHUNK]
        qc = q[r0:r0 + GLA_CHUNK]
        kc = k[r0:r0 + GLA_CHUNK]
        q_in = (qc * jnp.exp(bc - b_mid)).astype(_BF)
        k_in = (kc * jnp.exp(b_mid - bc)).astype(_BF)
        k_dec = (kc * jnp.exp(b_last - bc)).astype(_BF)
        qdec_l[c] = (qc * jnp.exp(bc)).astype(_BF)
        dec = jnp.exp(b_last)
        for hd in range(GLA_HEADS):
            ks = slice(hd * GLA_HEAD_K, (hd + 1) * GLA_HEAD_K)
            vs = slice(hd * GLA_HEAD_V, (hd + 1) * GLA_HEAD_V)
            sc = lax.dot_general(q_in[:, ks], k_in[:, ks], (((1,), (1,)), ((), ())),
                                 preferred_element_type=_F32)
            scores_l[c, hd] = jnp.where(causal, sc, 0.0).astype(_BF)
            kv_l[c, hd] = lax.dot_general(k_dec[:, ks], vb[r0:r0 + GLA_CHUNK, vs], (((0,), (0,)), ((), ())),
                                          preferred_element_type=_F32)
            dcol_l[c, hd] = jnp.sum(
                jnp.where(eye_k, jnp.broadcast_to(dec[:, ks], (GLA_HEAD_K, GLA_HEAD_K)), 0.0),
                axis=1, keepdims=True)

    sprev_l = {}
    for hd in range(GLA_HEADS):
        s_run = state_ref[hd]
        for c in range(n_chunks):
            sprev_l[c, hd] = s_run.astype(_BF)
            s_run = dcol_l[c, hd] * s_run + kv_l[c, hd]
        state_ref[hd] = s_run

    for hd in range(GLA_HEADS):
        for c in range(n_chunks):
            r0 = c * GLA_CHUNK
            ks = slice(hd * GLA_HEAD_K, (hd + 1) * GLA_HEAD_K)
            vs = slice(hd * GLA_HEAD_V, (hd + 1) * GLA_HEAD_V)
            lhs = jnp.concatenate([qdec_l[c][:, ks], scores_l[c, hd]], axis=1)
            rhs = jnp.concatenate([sprev_l[c, hd], vb[r0:r0 + GLA_CHUNK, vs]], axis=0)
            o_scr[r0:r0 + GLA_CHUNK, vs] = _dot(lhs, rhs)
    og = proj(_C_OG, _C_BA)
    gn_g = gn_g_ref[...]
    o_heads = []
    for hd in range(GLA_HEADS):
        oh = o_scr[:, hd * GLA_HEAD_V:(hd + 1) * GLA_HEAD_V]
        o_heads.append(_rms(oh, gn_g))
    o_n = jnp.concatenate(o_heads, axis=1)
    o_g = (o_n * (og * _sigmoid(og))).astype(_BF)

    half = ts // 2
    y_b = [_dot(o_g[r0:r0 + half], gla_wout_ref[...]) for r0 in (0, half)]
    for j, r0 in enumerate((0, half)):
        merged = (_sigmoid(bg[r0:r0 + half, :D_MODEL]) * y_a[r0:r0 + half]
                  + _sigmoid(bg[r0:r0 + half, D_MODEL:]) * y_b[j])
        out_ref[r0:r0 + half, :] = x[r0:r0 + half] + _dot(merged.astype(_BF), wo_ref[...])


def _ffn_kernel(x_ref, p_ref, g_ffn_ref, w_up_ref, w_down_ref, g_ple_ref, w_pg_ref, w_pp_ref,
                g_fin_ref, out_ref):
    x = x_ref[...]
    hb = _rms(x, g_ffn_ref[...]).astype(_BF)
    acts = []
    for f0 in range(0, D_FF, FFN_SLAB):
        up = jnp.maximum(_dot(hb, w_up_ref[:, f0:f0 + FFN_SLAB]), 0.0)
        acts.append((up * up).astype(_BF))
    acc = x + _dot(jnp.concatenate(acts, axis=1), w_down_ref[...])
    half = x_ref.shape[0] // 2
    for r0 in (0, half):
        acc_h = acc[r0:r0 + half]
        h3 = _rms(acc_h, g_ple_ref[...]).astype(_BF)
        gate = _sigmoid(_dot(h3, w_pg_ref[...]))
        ple = _dot(p_ref[r0:r0 + half, :].astype(_BF), w_pp_ref[...])
        out_ref[r0:r0 + half, :] = _rms(acc_h + gate * ple, g_fin_ref[...])


def _resident(shape):
    zeros = (0,) * len(shape)
    return pl.BlockSpec(shape, lambda *_: zeros, pipeline_mode=pl.Buffered(1))


def _slab_specs(arr, steps, step_of):
    rows = arr.shape[1]
    n = max(k for k in range(1, steps + 1) if rows % k == 0 and (rows // k) % BF16_SUBLANES == 0)
    in_spec = pl.BlockSpec((None, rows // n, arr.shape[2]), lambda *g: (0, jnp.minimum(step_of(*g), n - 1), 0))
    out_spec = pl.BlockSpec((rows // n, arr.shape[2]), lambda *g: (jnp.minimum(step_of(*g), n - 1), 0))
    return in_spec, out_spec


def _bf16_like(arr):
    return jax.ShapeDtypeStruct(arr.shape[1:], _BF)


def kernel(x, p, norm_mix_g, w_in, sg_ln_g, sg_ln_b, sg_w_s, sg_b_s, sg_w_out, gla_w_gate_up, gla_b_gate, gla_norm_g, gla_w_out, w_o, norm_ffn_g, ffn_w_up, ffn_w_down, ple_norm_g, ple_w_gate, ple_w_proj, final_norm_g):
    batch, seq, d = x.shape
    assert w_in.shape[0] == 1
    assert d == D_MODEL and seq % MIX_TILE == 0 and (batch * seq) % FFN_TILE == 0
    tokens = batch * seq
    i = 0

    mix_weights = (jnp.swapaxes(w_in, 1, 2), sg_w_out, gla_w_gate_up, gla_w_out, w_o)
    side_weights = (ffn_w_up, ffn_w_down, ple_w_gate, ple_w_proj)

    pos = np.arange(CUMSUM_ROWS)
    lmat = jnp.asarray((pos[:, None] // GLA_CHUNK == pos[None, :] // GLA_CHUNK) & (pos[None, :] <= pos[:, None]), _BF)

    n_seq = seq // MIX_TILE
    n_tiles = batch * n_seq

    def tile_of(step):
        return jnp.maximum(step - MIX_PREP_STEPS, 0)

    x_spec = pl.BlockSpec((None, MIX_TILE, D_MODEL), lambda s: (tile_of(s) // n_seq, tile_of(s) % n_seq, 0))
    prep_specs = [_slab_specs(a, MIX_PREP_STEPS, lambda s: s) for a in mix_weights]
    side_specs = [_slab_specs(a, n_tiles, tile_of) for a in side_weights]
    pre = (norm_mix_g[i].reshape(1, D_MODEL), lmat)
    post = (sg_ln_g[i].reshape(1, SG_WIDTH), sg_ln_b[i].reshape(1, SG_WIDTH), sg_w_s[i], sg_b_s[i],
            gla_b_gate[i].reshape(1, GLA_DK), gla_norm_g[i].reshape(1, GLA_HEAD_V))
    x, w_up_b, w_down_b, w_pg_b, w_pp_b = pl.pallas_call(
        functools.partial(_mix_kernel, n_seq=n_seq, prep_steps=MIX_PREP_STEPS),
        grid=(MIX_PREP_STEPS + n_tiles,),
        in_specs=[x_spec] + [_resident(a.shape) for a in pre] + [sp[0] for sp in prep_specs]
        + [_resident(a.shape) for a in post] + [sp[0] for sp in side_specs],
        out_specs=[x_spec] + [sp[1] for sp in side_specs],
        out_shape=[jax.ShapeDtypeStruct((batch, seq, D_MODEL), _F32)] + [_bf16_like(a) for a in side_weights],
        scratch_shapes=[
            pltpu.VMEM((GLA_HEADS, GLA_HEAD_K, GLA_HEAD_V), _F32),
            pltpu.VMEM((MIX_TILE, GLA_DV), _F32),
        ] + [pltpu.VMEM(a.shape[1:], _BF) for a in mix_weights],
        compiler_params=pltpu.CompilerParams(
            dimension_semantics=("arbitrary",),
            vmem_limit_bytes=VMEM_LIMIT_BYTES,
        ),
        name="token_mixing",
    )(x, *pre, *mix_weights, *post, *side_weights)

    ffn_inputs = (
        x.reshape(tokens, D_MODEL),
        p[i].reshape(tokens, PLE_DIM),
        norm_ffn_g[i].reshape(1, D_MODEL),
        w_up_b,
        w_down_b,
        ple_norm_g[i].reshape(1, D_MODEL),
        w_pg_b,
        w_pp_b,
        final_norm_g.reshape(1, D_MODEL),
    )
    tok_spec = pl.BlockSpec((FFN_TILE, D_MODEL), lambda t: (t, 0))
    out = pl.pallas_call(
        _ffn_kernel,
        grid=(tokens // FFN_TILE,),
        in_specs=[tok_spec, pl.BlockSpec((FFN_TILE, PLE_DIM), lambda t: (t, 0))]
        + [_resident(a.shape) for a in ffn_inputs[2:]],
        out_specs=tok_spec,
        out_shape=jax.ShapeDtypeStruct((tokens, D_MODEL), _F32),
        compiler_params=pltpu.CompilerParams(
            dimension_semantics=("arbitrary",),
            vmem_limit_bytes=VMEM_LIMIT_BYTES,
        ),
        name="channel_mixing",
    )(*ffn_inputs)
    return out.reshape(batch, seq, D_MODEL)
```

```python
import functools

import jax
import jax.numpy as jnp
import numpy as np
from jax import lax
from jax.experimental import pallas as pl
from jax.experimental.pallas import tpu as pltpu

D_MODEL = 1024
PLE_DIM = 256
SG_GROUPS = 4
SG_CHUNK = 128
SG_WIDTH = D_MODEL // 2
SG_GROUP_DIM = SG_WIDTH // SG_GROUPS
GLA_HEADS = 4
GLA_DK = D_MODEL // 2
GLA_DV = D_MODEL
GLA_HEAD_K = GLA_DK // GLA_HEADS
GLA_HEAD_V = GLA_DV // GLA_HEADS
GLA_GATE_RANK = 16
GLA_GATE_TEMP = 16.0
GLA_CHUNK = 64
D_FF = 4 * D_MODEL
EPS = 1e-6

_C_U = 0
_C_Q = _C_U + 2 * SG_WIDTH
_C_LR = _C_Q + 2 * GLA_DK + GLA_DV
_C_OG = _C_LR + GLA_GATE_RANK
_C_BA = _C_OG + GLA_DV
_C_END = _C_BA + 2 * D_MODEL

MIX_TILE = 512
CUMSUM_ROWS = 256
MIX_PREP_STEPS = 11
BF16_SUBLANES = 16
FFN_TILE = 1024
FFN_SLAB = 1024
VMEM_LIMIT_BYTES = 56 * 1024 * 1024

_BF = jnp.bfloat16
_F32 = jnp.float32

_dot = functools.partial(jnp.dot, preferred_element_type=_F32)


def _rms(x, g):
    ms = jnp.mean(x * x, axis=-1, keepdims=True)
    return x * lax.rsqrt(ms + EPS) * g


def _gelu_tanh(x):
    c = 0.7978845608028654
    return x * (0.5 * (1.0 + jnp.tanh(c * (x + 0.044715 * (x * x * x)))))


def _sigmoid(x):
    return 0.5 * jnp.tanh(0.5 * x) + 0.5


def _log_sigmoid(x):
    return jnp.minimum(x, 0.0) - jnp.log1p(jnp.exp(-jnp.abs(x)))


def _cast_blocks(src_refs, dst_refs):
    for src, dst in zip(src_refs, dst_refs):
        dst[...] = src[...].astype(_BF)


def _mix_kernel(x_ref, g_ref, lmat_ref, w_in_t_f, sg_wout_f, wgu_f, gla_wout_f, wo_f,
                ln_g_ref, ln_b_ref, ws_ref, bs_ref, bgate_ref, gn_g_ref,
                w_up_f, w_down_f, w_pg_f, w_pp_f,
                out_ref, w_up_o, w_down_o, w_pg_o, w_pp_o,
                state_ref, o_scr, w_in_t_ref, sg_wout_ref, wgu_ref, gla_wout_ref, wo_ref, *, n_seq, prep_steps):
    step = pl.program_id(0)

    @pl.when(step < prep_steps)
    def _():
        for src, dst in ((w_in_t_f, w_in_t_ref), (sg_wout_f, sg_wout_ref), (wgu_f, wgu_ref),
                         (gla_wout_f, gla_wout_ref), (wo_f, wo_ref)):
            rows = src.shape[0]
            n_slabs = dst.shape[0] // rows
            if n_slabs == 1:
                dst[...] = src[...].astype(_BF)
            else:
                r0 = pl.multiple_of(jnp.minimum(step, n_slabs - 1) * rows, rows)
                dst[pl.ds(r0, rows), :] = src[...].astype(_BF)

    @pl.when(step >= prep_steps)
    def _():
        _mix_tile((step - prep_steps) % n_seq == 0,
                  x_ref, g_ref, lmat_ref, w_in_t_ref, ln_g_ref, ln_b_ref, ws_ref, bs_ref,
                  sg_wout_ref, wgu_ref, bgate_ref, gn_g_ref, gla_wout_ref, wo_ref,
                  w_up_f, w_down_f, w_pg_f, w_pp_f,
                  out_ref, w_up_o, w_down_o, w_pg_o, w_pp_o, state_ref, o_scr)


def _mix_tile(first_of_sequence, x_ref, g_ref, lmat_ref, w_in_t_ref, ln_g_ref, ln_b_ref, ws_ref, bs_ref,
              sg_wout_ref, wgu_ref, bgate_ref, gn_g_ref, gla_wout_ref, wo_ref,
              w_up_f, w_down_f, w_pg_f, w_pp_f,
              out_ref, w_up_o, w_down_o, w_pg_o, w_pp_o, state_ref, o_scr):
    ts = x_ref.shape[0]

    def proj(lo, hi):
        return lax.dot_general(hb, w_in_t_ref[lo:hi, :], (((1,), (1,)), ((), ())), preferred_element_type=_F32)

    @pl.when(first_of_sequence)
    def _():
        state_ref[...] = jnp.zeros_like(state_ref)

    x = x_ref[...]
    hb = _rms(x, g_ref[...]).astype(_BF)

    glr = proj(_C_LR, _C_OG)
    uv = proj(_C_U, _C_Q)
    zg = _dot(glr.astype(_BF), wgu_ref[...]) + bgate_ref[...]
    qkv = proj(_C_Q, _C_LR)

    log_a = _log_sigmoid(zg) * (1.0 / GLA_GATE_TEMP)
    la_hi = log_a.astype(_BF)
    la_lo = (log_a - la_hi.astype(_F32)).astype(_BF)
    lmat2 = jnp.concatenate([lmat_ref[...], lmat_ref[...]], axis=1)
    b = jnp.concatenate(
        [_dot(lmat2, jnp.concatenate([la_hi[r0:r0 + CUMSUM_ROWS], la_lo[r0:r0 + CUMSUM_ROWS]], axis=0))
         for r0 in range(0, ts, CUMSUM_ROWS)], axis=0)

    u = _gelu_tanh(uv[:, :SG_WIDTH])
    v = _gelu_tanh(uv[:, SG_WIDTH:])
    mu = jnp.mean(v, axis=-1, keepdims=True)
    vc = v - mu
    var = jnp.mean(vc * vc, axis=-1, keepdims=True)
    vn = (vc * lax.rsqrt(var + EPS) * ln_g_ref[...] + ln_b_ref[...]).astype(_BF)

    bg = proj(_C_BA, _C_END)

    _cast_blocks((w_up_f, w_down_f, w_pg_f, w_pp_f), (w_up_o, w_down_o, w_pg_o, w_pp_o))

    row = lax.broadcasted_iota(jnp.int32, (SG_CHUNK, SG_CHUNK), 0)
    col = lax.broadcasted_iota(jnp.int32, (SG_CHUNK, SG_CHUNK), 1)
    tril = row >= col
    w_sp = [jnp.where(tril, ws_ref[g], 0.0).astype(_BF) for g in range(SG_GROUPS)]
    bias = jnp.concatenate(
        [jnp.broadcast_to(
            jnp.sum(jnp.where(row == col, jnp.broadcast_to(bs_ref[g:g + 1, :], (SG_CHUNK, SG_CHUNK)), 0.0),
                    axis=1, keepdims=True), (SG_CHUNK, SG_GROUP_DIM)) for g in range(SG_GROUPS)], axis=1)
    sg_chunks = ts // SG_CHUNK
    per_group = []
    for g in range(SG_GROUPS):
        gs = slice(g * SG_GROUP_DIM, (g + 1) * SG_GROUP_DIM)
        v_wide = jnp.concatenate([vn[c * SG_CHUNK:(c + 1) * SG_CHUNK, gs] for c in range(sg_chunks)], axis=1)
        per_group.append(_dot(w_sp[g], v_wide))
    mixed = jnp.concatenate(
        [jnp.concatenate([per_group[g][:, c * SG_GROUP_DIM:(c + 1) * SG_GROUP_DIM] for g in range(SG_GROUPS)], axis=1)
         + bias for c in range(sg_chunks)], axis=0)
    z = (u * mixed).astype(_BF)
    y_a = _dot(z, sg_wout_ref[...])

    q = qkv[:, :GLA_DK] * (GLA_HEAD_K ** -0.5)
    k = qkv[:, GLA_DK:2 * GLA_DK]
    vb = qkv[:, 2 * GLA_DK:].astype(_BF)
    r64 = lax.broadcasted_iota(jnp.int32, (GLA_CHUNK, GLA_CHUNK), 0)
    c64 = lax.broadcasted_iota(jnp.int32, (GLA_CHUNK, GLA_CHUNK), 1)
    causal = r64 >= c64
    rk = lax.broadcasted_iota(jnp.int32, (GLA_HEAD_K, GLA_HEAD_K), 0)
    ck = lax.broadcasted_iota(jnp.int32, (GLA_HEAD_K, GLA_HEAD_K), 1)
    eye_k = rk == ck
    n_chunks = ts // GLA_CHUNK
    scores_l, kv_l, qdec_l, dcol_l = {}, {}, {}, {}
    for c in range(n_chunks):
        r0 = c * GLA_CHUNK
        bc = b[r0:r0 + GLA_CHUNK]
        b_mid = bc[GLA_CHUNK // 2 - 1:GLA_CHUNK // 2]
        b_last = bc[GLA_CHUNK - 1:GLA_CHUNK]
        qc = q[r0:r0 + GLA_CHUNK]
        kc = k[r0:r0 + GLA_CHUNK]
        q_in = (qc * jnp.exp(bc - b_mid)).astype(_BF)
        k_in = (kc * jnp.exp(b_mid - bc)).astype(_BF)
        k_dec = (kc * jnp.exp(b_last - bc)).astype(_BF)
        qdec_l[c] = (qc * jnp.exp(bc)).astype(_BF)
        dec = jnp.exp(b_last)
        for hd in range(GLA_HEADS):
            ks = slice(hd * GLA_HEAD_K, (hd + 1) * GLA_HEAD_K)
            vs = slice(hd * GLA_HEAD_V, (hd + 1) * GLA_HEAD_V)
            sc = lax.dot_general(q_in[:, ks], k_in[:, ks], (((1,), (1,)), ((), ())),
                                 preferred_element_type=_F32)
            scores_l[c, hd] = jnp.where(causal, sc, 0.0).astype(_BF)
            kv_l[c, hd] = lax.dot_general(k_dec[:, ks], vb[r0:r0 + GLA_CHUNK, vs], (((0,), (0,)), ((), ())),
                                          preferred_element_type=_F32)
            dcol_l[c, hd] = jnp.sum(
                jnp.where(eye_k, jnp.broadcast_to(dec[:, ks], (GLA_HEAD_K, GLA_HEAD_K)), 0.0),
                axis=1, keepdims=True)

    sprev_l = {}
    for hd in range(GLA_HEADS):
        s_run = state_ref[hd]
        for c in range(n_chunks):
            sprev_l[c, hd] = s_run.astype(_BF)
            s_run = dcol_l[c, hd] * s_run + kv_l[c, hd]
        state_ref[hd] = s_run

    for hd in range(GLA_HEADS):
        for c in range(n_chunks):
            r0 = c * GLA_CHUNK
            ks = slice(hd * GLA_HEAD_K, (hd + 1) * GLA_HEAD_K)
            vs = slice(hd * GLA_HEAD_V, (hd + 1) * GLA_HEAD_V)
            lhs = jnp.concatenate([qdec_l[c][:, ks], scores_l[c, hd]], axis=1)
            rhs = jnp.concatenate([sprev_l[c, hd], vb[r0:r0 + GLA_CHUNK, vs]], axis=0)
            o_scr[r0:r0 + GLA_CHUNK, vs] = _dot(lhs, rhs)
    og = proj(_C_OG, _C_BA)
    gn_g = gn_g_ref[...]
    o_heads = []
    for hd in range(GLA_HEADS):
        oh = o_scr[:, hd * GLA_HEAD_V:(hd + 1) * GLA_HEAD_V]
        o_heads.append(_rms(oh, gn_g))
    o_n = jnp.concatenate(o_heads, axis=1)
    o_g = (o_n * (og * _sigmoid(og))).astype(_BF)

    y_b = _dot(o_g, gla_wout_ref[...])
    merged = _sigmoid(bg[:, :D_MODEL]) * y_a + _sigmoid(bg[:, D_MODEL:]) * y_b
    out_ref[...] = x + _dot(merged.astype(_BF), wo_ref[...])


def _ffn_kernel(x_ref, p_ref, g_ffn_ref, w_up_ref, w_down_ref, g_ple_ref, w_pg_ref, w_pp_ref,
                g_fin_ref, out_ref):
    x = x_ref[...]
    hb = _rms(x, g_ffn_ref[...]).astype(_BF)
    acts = []
    for f0 in range(0, D_FF, FFN_SLAB):
        up = jnp.maximum(_dot(hb, w_up_ref[:, f0:f0 + FFN_SLAB]), 0.0)
        acts.append((up * up).astype(_BF))
    acc = x + _dot(jnp.concatenate(acts, axis=1), w_down_ref[...])
    half = x_ref.shape[0] // 2
    for r0 in (0, half):
        acc_h = acc[r0:r0 + half]
        h3 = _rms(acc_h, g_ple_ref[...]).astype(_BF)
        gate = _sigmoid(_dot(h3, w_pg_ref[...]))
        ple = _dot(p_ref[r0:r0 + half, :].astype(_BF), w_pp_ref[...])
        out_ref[r0:r0 + half, :] = _rms(acc_h + gate * ple, g_fin_ref[...])


def _resident(shape):
    zeros = (0,) * len(shape)
    return pl.BlockSpec(shape, lambda *_: zeros, pipeline_mode=pl.Buffered(1))


def _slab_specs(arr, steps, step_of):
    rows = arr.shape[1]
    n = max(k for k in range(1, steps + 1) if rows % k == 0 and (rows // k) % BF16_SUBLANES == 0)
    in_spec = pl.BlockSpec((None, rows // n, arr.shape[2]), lambda *g: (0, jnp.minimum(step_of(*g), n - 1), 0))
    out_spec = pl.BlockSpec((rows // n, arr.shape[2]), lambda *g: (jnp.minimum(step_of(*g), n - 1), 0))
    return in_spec, out_spec


def _bf16_like(arr):
    return jax.ShapeDtypeStruct(arr.shape[1:], _BF)


def kernel(x, p, norm_mix_g, w_in, sg_ln_g, sg_ln_b, sg_w_s, sg_b_s, sg_w_out, gla_w_gate_up, gla_b_gate, gla_norm_g, gla_w_out, w_o, norm_ffn_g, ffn_w_up, ffn_w_down, ple_norm_g, ple_w_gate, ple_w_proj, final_norm_g):
    batch, seq, d = x.shape
    assert w_in.shape[0] == 1
    assert d == D_MODEL and seq % MIX_TILE == 0 and (batch * seq) % FFN_TILE == 0
    tokens = batch * seq
    i = 0

    mix_weights = (jnp.swapaxes(w_in, 1, 2), sg_w_out, gla_w_gate_up, gla_w_out, w_o)
    side_weights = (ffn_w_up, ffn_w_down, ple_w_gate, ple_w_proj)

    pos = np.arange(CUMSUM_ROWS)
    lmat = jnp.asarray((pos[:, None] // GLA_CHUNK == pos[None, :] // GLA_CHUNK) & (pos[None, :] <= pos[:, None]), _BF)

    n_seq = seq // MIX_TILE
    n_tiles = batch * n_seq

    def tile_of(step):
        return jnp.maximum(step - MIX_PREP_STEPS, 0)

    x_spec = pl.BlockSpec((None, MIX_TILE, D_MODEL), lambda s: (tile_of(s) // n_seq, tile_of(s) % n_seq, 0))
    prep_specs = [_slab_specs(a, MIX_PREP_STEPS, lambda s: s) for a in mix_weights]
    side_specs = [_slab_specs(a, n_tiles, tile_of) for a in side_weights]
    pre = (norm_mix_g[i].reshape(1, D_MODEL), lmat)
    post = (sg_ln_g[i].reshape(1, SG_WIDTH), sg_ln_b[i].reshape(1, SG_WIDTH), sg_w_s[i], sg_b_s[i],
            gla_b_gate[i].reshape(1, GLA_DK), gla_norm_g[i].reshape(1, GLA_HEAD_V))
    x, w_up_b, w_down_b, w_pg_b, w_pp_b = pl.pallas_call(
        functools.partial(_mix_kernel, n_seq=n_seq, prep_steps=MIX_PREP_STEPS),
        grid=(MIX_PREP_STEPS + n_tiles,),
        in_specs=[x_spec] + [_resident(a.shape) for a in pre] + [sp[0] for sp in prep_specs]
        + [_resident(a.shape) for a in post] + [sp[0] for sp in side_specs],
        out_specs=[x_spec] + [sp[1] for sp in side_specs],
        out_shape=[jax.ShapeDtypeStruct((batch, seq, D_MODEL), _F32)] + [_bf16_like(a) for a in side_weights],
        scratch_shapes=[
            pltpu.VMEM((GLA_HEADS, GLA_HEAD_K, GLA_HEAD_V), _F32),
            pltpu.VMEM((MIX_TILE, GLA_DV), _F32),
        ] + [pltpu.VMEM(a.shape[1:], _BF) for a in mix_weights],
        compiler_params=pltpu.CompilerParams(
            dimension_semantics=("arbitrary",),
            vmem_limit_bytes=VMEM_LIMIT_BYTES,
        ),
        name="token_mixing",
    )(x, *pre, *mix_weights, *post, *side_weights)

    ffn_inputs = (
        x.reshape(tokens, D_MODEL),
        p[i].reshape(tokens, PLE_DIM),
        norm_ffn_g[i].reshape(1, D_MODEL),
        w_up_b,
        w_down_b,
        ple_norm_g[i].reshape(1, D_MODEL),
        w_pg_b,
        w_pp_b,
        final_norm_g.reshape(1, D_MODEL),
    )
    tok_spec = pl.BlockSpec((FFN_TILE, D_MODEL), lambda t: (t, 0))
    out = pl.pallas_call(
        _ffn_kernel,
        grid=(tokens // FFN_TILE,),
        in_specs=[tok_spec, pl.BlockSpec((FFN_TILE, PLE_DIM), lambda t: (t, 0))]
        + [_resident(a.shape) for a in ffn_inputs[2:]],
        out_specs=tok_spec,
        out_shape=jax.ShapeDtypeStruct((tokens, D_MODEL), _F32),
        compiler_params=pltpu.CompilerParams(
            dimension_semantics=("arbitrary",),
            vmem_limit_bytes=VMEM_LIMIT_BYTES,
        ),
        name="channel_mixing",
    )(*ffn_inputs)
    return out.reshape(batch, seq, D_MODEL)
```
